```python
import math
import jax
import jax.numpy as jnp
from jax import lax
import numpy as np

D_MODEL = 1024
BATCH = 2
SEQ = 8192
DEPTH = 2
DEC_BATCH = 8
DEC_SEQ = 4096
PAST_LEN = 128

N_EVEN = (DEPTH + 1) // 2
N_ODD = DEPTH // 2
RMS_EPS = 1e-6
F32 = jnp.float32

SSD_HEAD_DIM = 64
SSD_HEADS = 24
SSD_INNER = SSD_HEADS * SSD_HEAD_DIM
SSD_GROUPS = 4
SSD_STATE = 128
SSD_CONV_DIM = SSD_INNER + 2 * SSD_GROUPS * SSD_STATE
SSD_CONV_TAPS = 5
SSD_CHUNK = 128
POOL_WINDOWS = (2, 4, 8, 16)
POOL_GROUPS = 4
POOL_GROUP_DIM = 128
POOL_WIDTH = POOL_GROUPS * POOL_GROUP_DIM
EVEN_IN = SSD_INNER + SSD_CONV_DIM + 2 * SSD_HEADS + POOL_WIDTH
EVEN_MIX = SSD_INNER + POOL_WIDTH
SC_WIDTH = 1024
SC_TAPS = 3
N_Q_HEADS = 16
N_KV_HEADS = 4
HEAD_DIM = 64
Q_REP = N_Q_HEADS // N_KV_HEADS
WINDOW = 128
ATTN_BLOCK = 128
ROPE_THETA = 10000.0
Q_DIM = N_Q_HEADS * HEAD_DIM
KV_DIM = N_KV_HEADS * HEAD_DIM
ODD_IN = 3 * SC_WIDTH + Q_DIM + 2 * KV_DIM
ODD_MIX = SC_WIDTH + Q_DIM
N_EXPERT_GROUPS = 4
EXPERTS_PER_GROUP = 8
N_EXPERTS = N_EXPERT_GROUPS * EXPERTS_PER_GROUP
TOP_K = 2
D_EXPERT = 512
MOE_BLOCK = 256

kernel_name = 'hybrid_bidir_ssd_pool_shortconv_swa_hmoe'


def rmsnorm(x, w):
    xf = x.astype(F32)
    y = xf * lax.rsqrt(jnp.mean(xf * xf, axis=-1, keepdims=True) + RMS_EPS)
    return (y * w.astype(F32)).astype(x.dtype)


def dwconv_centred(x, w):
    taps, c = w.shape
    return lax.conv_general_dilated(
        x, w[:, None, :].astype(x.dtype), window_strides=(1,),
        padding=[(taps // 2, taps // 2)], dimension_numbers=('NWC', 'WIO', 'NWC'),
        feature_group_count=c)


def ssd_scan_one(x, dt, a, bm, cm):
    l, h, p = x.shape
    g, n = bm.shape[1], bm.shape[2]
    e = h // g
    q = SSD_CHUNK
    c = l // q
    x = x.astype(F32).reshape(c, q, g, e, p)
    dt = dt.reshape(c, q, g, e)
    bm = bm.astype(F32).reshape(c, q, g, n)
    cm = cm.astype(F32).reshape(c, q, g, n)
    a_cs = jnp.cumsum(dt * a.reshape(g, e), axis=1)
    xdt = x * dt[..., None]
    causal = jnp.tril(jnp.ones((q, q), bool))[None, :, :, None, None]
    seg = a_cs[:, :, None] - a_cs[:, None, :]
    decay = jnp.exp(jnp.where(causal, seg, -jnp.inf))
    cb = jnp.einsum('cign,cjgn->cijg', cm, bm)
    y_diag = jnp.einsum('cijge,cjgep->cigep', cb[..., None] * decay, xdt)
    decay_end = jnp.exp(a_cs[:, -1:] - a_cs)
    states = jnp.einsum('cjgn,cjgep->cgepn', bm, xdt * decay_end[..., None])
    chunk_decay = jnp.exp(a_cs[:, -1])

    def carry_step(state, inp):
        s_c, d_c = inp
        return state * d_c[..., None, None] + s_c, state

    _, h_in = lax.scan(carry_step, jnp.zeros((g, e, p, n), F32), (states, chunk_decay))
    y_off = jnp.einsum('cign,cgepn->cigep', cm, h_in) * jnp.exp(a_cs)[..., None]
    return (y_diag + y_off).reshape(l, h, p)


def centred_mean(x, w):
    b, l, c = x.shape
    cs = jnp.concatenate([jnp.zeros((b, 1, c), x.dtype), jnp.cumsum(x, axis=1)], axis=1)
    t = jnp.arange(l)
    lo = jnp.clip(t - w // 2, 0, l)
    hi = jnp.clip(t - w // 2 + w, 0, l)
    return (cs[:, hi] - cs[:, lo]) / (hi - lo).astype(x.dtype)[None, :, None]


def pool_mixer(xp, pool_w, pool_scale):
    b, l, _ = xp.shape
    xg = xp.astype(F32).reshape(b, l, POOL_GROUPS, POOL_GROUP_DIM)
    means = jnp.stack([centred_mean(xg[:, :, i], w) for i, w in enumerate(POOL_WINDOWS)], axis=2)
    out = jnp.einsum('blgc,gcd->blgd', means - xg, pool_w.astype(F32))
    out = out * pool_scale.astype(F32).reshape(POOL_GROUPS, POOL_GROUP_DIM)
    return out.reshape(b, l, POOL_WIDTH).astype(xp.dtype)


def even_mixer(h, w_in, conv_w, conv_b, a_log, dt_bias, d_skip, gnorm_w, pool_w, pool_scale, w_out):
    b, l, _ = h.shape
    proj = h @ w_in
    z, xbc, dt_raw, xp = jnp.split(
        proj, [SSD_INNER, SSD_INNER + SSD_CONV_DIM, SSD_INNER + SSD_CONV_DIM + 2 * SSD_HEADS], axis=-1)
    xbc = jax.nn.silu(dwconv_centred(xbc, conv_w) + conv_b.astype(xbc.dtype))
    xs, bm, cm = jnp.split(xbc, [SSD_INNER, SSD_INNER + SSD_GROUPS * SSD_STATE], axis=-1)
    xs = xs.reshape(b, l, SSD_HEADS, SSD_HEAD_DIM)
    bm = bm.reshape(b, l, SSD_GROUPS, SSD_STATE)
    cm = cm.reshape(b, l, SSD_GROUPS, SSD_STATE)
    dt = jax.nn.softplus(dt_raw.astype(F32).reshape(b, l, 2, SSD_HEADS) + dt_bias.astype(F32))
    a = -jnp.exp(a_log.astype(F32))

    def bidir(args):
        xs_, dtf, dtb, b_, c_ = args
        rev = lambda t: jnp.flip(t, axis=0)
        y_f = ssd_scan_one(xs_, dtf, a[0], b_, c_)
        y_b = rev(ssd_scan_one(rev(xs_), rev(dtb), a[1], rev(b_), rev(c_)))
        return y_f + y_b

    y = lax.map(bidir, (xs, dt[:, :, 0], dt[:, :, 1], bm, cm))
    y = y + xs.astype(F32) * d_skip.astype(F32)[:, None]
    y = y.reshape(b, l, SSD_INNER) * jax.nn.silu(z.astype(F32))
    y_ssd = rmsnorm(y, gnorm_w).astype(h.dtype)
    y_pool = pool_mixer(xp, pool_w, pool_scale)
    return jnp.concatenate([y_ssd, y_pool], axis=-1) @ w_out


def rope_tables(l):
    inv = 1.0 / (ROPE_THETA ** (jnp.arange(0, HEAD_DIM, 2, dtype=F32) / HEAD_DIM))
    ang = jnp.arange(l, dtype=F32)[:, None] * inv[None, :]
    return jnp.cos(ang), jnp.sin(ang)


def apply_rope(t, cos, sin):
    half = HEAD_DIM // 2
    t1 = t[..., :half].astype(F32)
    t2 = t[..., half:].astype(F32)
    c = cos[None, :, None, :]
    s = sin[None, :, None, :]
    return jnp.concatenate([t1 * c - t2 * s, t2 * c + t1 * s], axis=-1).astype(t.dtype)


def band_attention_one(q, k, v, sinks):
    l = q.shape[0]
    nb = l // ATTN_BLOCK
    qb = q.reshape(nb, ATTN_BLOCK, N_KV_HEADS, Q_REP, HEAD_DIM)

    def band(t):
        tp = jnp.pad(t, ((ATTN_BLOCK, ATTN_BLOCK), (0, 0), (0, 0)))
        tp = tp.reshape(nb + 2, ATTN_BLOCK, N_KV_HEADS, HEAD_DIM)
        return jnp.concatenate([tp[:-2], tp[1:-1], tp[2:]], axis=1)

    s = jnp.einsum('nqhrd,nkhd->nhrqk', qb, band(k)).astype(F32) * (HEAD_DIM ** -0.5)
    qpos = (jnp.arange(nb) * ATTN_BLOCK)[:, None] + jnp.arange(ATTN_BLOCK)[None, :]
    kpos = (jnp.arange(nb) * ATTN_BLOCK - ATTN_BLOCK)[:, None] + jnp.arange(3 * ATTN_BLOCK)[None, :]
    diff = qpos[:, :, None] - kpos[:, None, :]
    mask = (jnp.abs(diff) <= WINDOW) & (kpos >= 0)[:, None, :] & (kpos < l)[:, None, :]
    s = jnp.where(mask[:, None, None], s, -jnp.inf)
    sink = sinks.astype(F32).reshape(N_KV_HEADS, Q_REP)[None, :, :, None, None]
    m = jnp.maximum(jnp.max(s, axis=-1, keepdims=True), sink)
    p = jnp.exp(s - m)
    p = p / (jnp.sum(p, axis=-1, keepdims=True) + jnp.exp(sink - m))
    o = jnp.einsum('nhrqk,nkhd->nqhrd', p.astype(v.dtype), band(v))
    return o.reshape(l, Q_DIM)


def odd_mixer(h, w_in, conv_w, sinks, w_out):
    b, l, _ = h.shape
    proj = h @ w_in
    g_b, g_c, xc, q, k, v = jnp.split(
        proj, [SC_WIDTH, 2 * SC_WIDTH, 3 * SC_WIDTH, 3 * SC_WIDTH + Q_DIM, 3 * SC_WIDTH + Q_DIM + KV_DIM],
        axis=-1)
    y_conv = g_b * dwconv_centred(g_c * xc, conv_w)
    cos, sin = rope_tables(l)
    q = apply_rope(q.reshape(b, l, N_Q_HEADS, HEAD_DIM), cos, sin)
    k = apply_rope(k.reshape(b, l, N_KV_HEADS, HEAD_DIM), cos, sin)
    v = v.reshape(b, l, N_KV_HEADS, HEAD_DIM)
    y_attn = lax.map(lambda args: band_attention_one(args[0], args[1], args[2], sinks), (q, k, v))
    return jnp.concatenate([y_conv, y_attn.astype(h.dtype)], axis=-1) @ w_out


def routed_experts(t, eidx, weights, w1, w3, w2):
    n_tok, d = t.shape
    n_assign = n_tok * TOP_K
    flat_e = eidx.reshape(-1)
    order = jnp.argsort(flat_e)
    se = flat_e[order]
    sorted_tok = (jnp.arange(n_assign) // TOP_K)[order]
    counts = jnp.bincount(flat_e, length=N_EXPERTS)
    padded = (counts + MOE_BLOCK - 1) // MOE_BLOCK * MOE_BLOCK
    pad_end = jnp.cumsum(padded)
    pad_start = pad_end - padded
    start = jnp.cumsum(counts) - counts
    dest = pad_start[se] + jnp.arange(n_assign) - start[se]
    n_blocks = -(-n_assign // MOE_BLOCK) + N_EXPERTS
    n_slots = n_blocks * MOE_BLOCK
    slot_tok = jnp.full((n_slots,), n_tok, jnp.int32).at[dest].set(sorted_tok.astype(jnp.int32))
    tp = jnp.concatenate([t, jnp.zeros((1, d), t.dtype)], axis=0)
    xin = tp[slot_tok].reshape(n_blocks, MOE_BLOCK, d)
    block_e = jnp.minimum(
        jnp.searchsorted(pad_end, jnp.arange(n_blocks) * MOE_BLOCK, side='right'), N_EXPERTS - 1)

    def expert_block(args):
        xb, e = args
        return (jax.nn.silu(xb @ w1[e]) * (xb @ w3[e])) @ w2[e]

    yout = lax.map(expert_block, (xin, block_e)).reshape(n_slots, d)
    contrib = yout[dest] * weights.reshape(-1)[order][:, None].astype(yout.dtype)
    return jnp.zeros((n_tok, d), yout.dtype).at[sorted_tok].add(contrib)


def hier_moe(h, w_group, b_group, w_expert, b_expert, w1, w3, w2):
    bsz, l, d = h.shape
    t = h.reshape(-1, d)
    n_tok = t.shape[0]
    rows = jnp.arange(n_tok)
    g_logits = (t @ w_group).astype(F32) + b_group.astype(F32)
    g_prob = jax.nn.softmax(g_logits, axis=-1)
    g_sel = jnp.argmax(g_logits, axis=-1)
    p_group = g_prob[rows, g_sel]
    e_logits = ((t @ w_expert).astype(F32) + b_expert.astype(F32)).reshape(
        n_tok, N_EXPERT_GROUPS, EXPERTS_PER_GROUP)[rows, g_sel]
    e_prob = jax.nn.softmax(e_logits, axis=-1)
    top_v, top_i = lax.top_k(e_prob, TOP_K)
    top_v = top_v / jnp.sum(top_v, axis=-1, keepdims=True)
    weights = top_v * p_group[:, None]
    eidx = g_sel[:, None] * EXPERTS_PER_GROUP + top_i
    return routed_experts(t, eidx, weights, w1, w3, w2).reshape(bsz, l, d)


def trunk(x, params):
    (norm_mix, norm_ffn, norm_final, even_w_in, ssd_conv_w, ssd_conv_b, ssd_A_log, ssd_dt_bias,
     ssd_D, ssd_norm_w, pool_w, pool_scale, even_w_out, odd_w_in, sc_conv_w, attn_sinks, odd_w_out,
     moe_w_group, moe_b_group, moe_w_expert, moe_b_expert, moe_w1, moe_w3, moe_w2) = params
    h = x
    for i in range(DEPTH):
        j = i // 2
        hn = rmsnorm(h, norm_mix[i])
        if i % 2 == 0:
            mix = even_mixer(hn, even_w_in[j], ssd_conv_w[j], ssd_conv_b[j], ssd_A_log[j], ssd_dt_bias[j],
                             ssd_D[j], ssd_norm_w[j], pool_w[j], pool_scale[j], even_w_out[j])
        else:
            mix = odd_mixer(hn, odd_w_in[j], sc_conv_w[j], attn_sinks[j], odd_w_out[j])
        h = h + mix.astype(h.dtype)
        h = h + hier_moe(rmsnorm(h, norm_ffn[i]), moe_w_group[i], moe_b_group[i], moe_w_expert[i],
                         moe_b_expert[i], moe_w1[i], moe_w3[i], moe_w2[i]).astype(h.dtype)
    return rmsnorm(h, norm_final)


def setup_inputs(seed: int = 0) -> dict:
    key = jax.random.key(seed)
    ks = iter(jax.random.split(key, 40))

    def nrm(shape, scale):
        return jax.random.normal(next(ks), shape, F32) * scale

    def gain(shape):
        return 1.0 + 0.1 * jax.random.normal(next(ks), shape, F32)

    x_prompt = nrm((BATCH, SEQ, D_MODEL), 1.0)
    x_sample = nrm((DEC_BATCH, DEC_SEQ, D_MODEL), 1.0)
    norm_mix = gain((DEPTH, D_MODEL))
    norm_ffn = gain((DEPTH, D_MODEL))
    norm_final = gain((D_MODEL,))
    even_w_in = nrm((N_EVEN, D_MODEL, EVEN_IN), D_MODEL ** -0.5)
    ssd_conv_w = nrm((N_EVEN, SSD_CONV_TAPS, SSD_CONV_DIM), SSD_CONV_TAPS ** -0.5)
    ssd_conv_b = nrm((N_EVEN, SSD_CONV_DIM), 0.02)
    ssd_A_log = jnp.log(jax.random.uniform(next(ks), (N_EVEN, 2, SSD_HEADS), F32, 1.0, 16.0))
    u = jax.random.uniform(next(ks), (N_EVEN, 2, SSD_HEADS), F32)
    dt0 = jnp.exp(u * (math.log(0.1) - math.log(0.001)) + math.log(0.001))
    ssd_dt_bias = dt0 + jnp.log(-jnp.expm1(-dt0))
    ssd_D = gain((N_EVEN, SSD_HEADS))
    ssd_norm_w = gain((N_EVEN, SSD_INNER))
    pool_w = nrm((N_EVEN, POOL_GROUPS, POOL_GROUP_DIM, POOL_GROUP_DIM), POOL_GROUP_DIM ** -0.5)
    pool_scale = gain((N_EVEN, POOL_WIDTH))
    even_w_out = nrm((N_EVEN, EVEN_MIX, D_MODEL), EVEN_MIX ** -0.5)
    odd_w_in = nrm((N_ODD, D_MODEL, ODD_IN), D_MODEL ** -0.5)
    sc_conv_w = nrm((N_ODD, SC_TAPS, SC_WIDTH), SC_TAPS ** -0.5)
    attn_sinks = nrm((N_ODD, N_Q_HEADS), 0.5)
    odd_w_out = nrm((N_ODD, ODD_MIX, D_MODEL), ODD_MIX ** -0.5)
    moe_w_group = nrm((DEPTH, D_MODEL, N_EXPERT_GROUPS), D_MODEL ** -0.5)
    moe_b_group = nrm((DEPTH, N_EXPERT_GROUPS), 0.01)
    moe_w_expert = nrm((DEPTH, D_MODEL, N_EXPERTS), D_MODEL ** -0.5)
    moe_b_expert = nrm((DEPTH, N_EXPERTS), 0.01)
    moe_w1 = nrm((DEPTH, N_EXPERTS, D_MODEL, D_EXPERT), D_MODEL ** -0.5)
    moe_w3 = nrm((DEPTH, N_EXPERTS, D_MODEL, D_EXPERT), D_MODEL ** -0.5)
    moe_w2 = nrm((DEPTH, N_EXPERTS, D_EXPERT, D_MODEL), D_EXPERT ** -0.5)
    return {'x_prompt': x_prompt, 'x_sample': x_sample, 'norm_mix': norm_mix, 'norm_ffn': norm_ffn,
            'norm_final': norm_final, 'even_w_in': even_w_in, 'ssd_conv_w': ssd_conv_w,
            'ssd_conv_b': ssd_conv_b, 'ssd_A_log': ssd_A_log, 'ssd_dt_bias': ssd_dt_bias, 'ssd_D': ssd_D,
            'ssd_norm_w': ssd_norm_w, 'pool_w': pool_w, 'pool_scale': pool_scale,
            'even_w_out': even_w_out, 'odd_w_in': odd_w_in, 'sc_conv_w': sc_conv_w,
            'attn_sinks': attn_sinks, 'odd_w_out': odd_w_out, 'moe_w_group': moe_w_group,
            'moe_b_group': moe_b_group, 'moe_w_expert': moe_w_expert, 'moe_b_expert': moe_b_expert,
            'moe_w1': moe_w1, 'moe_w3': moe_w3, 'moe_w2': moe_w2}


def reference(x_prompt, x_sample, norm_mix, norm_ffn, norm_final, even_w_in, ssd_conv_w, ssd_conv_b,
              ssd_A_log, ssd_dt_bias, ssd_D, ssd_norm_w, pool_w, pool_scale, even_w_out, odd_w_in,
              sc_conv_w, attn_sinks, odd_w_out, moe_w_group, moe_b_group, moe_w_expert, moe_b_expert,
              moe_w1, moe_w3, moe_w2):
    params = (norm_mix, norm_ffn, norm_final, even_w_in, ssd_conv_w, ssd_conv_b, ssd_A_log, ssd_dt_bias,
              ssd_D, ssd_norm_w, pool_w, pool_scale, even_w_out, odd_w_in, sc_conv_w, attn_sinks,
              odd_w_out, moe_w_group, moe_b_group, moe_w_expert, moe_b_expert, moe_w1, moe_w3, moe_w2)
    y_prompt = trunk(x_prompt, params)
    y_sample = trunk(x_sample, params)
    return (y_prompt, y_sample)
```

```python
import functools

import jax
import jax.numpy as jnp
from jax import lax
from jax.experimental import pallas as pl
from jax.experimental.pallas import tpu as pltpu

F32 = jnp.float32
BF16 = jnp.bfloat16
I32 = jnp.int32

RMS_EPS = 1e-6
D_MODEL = 1024
LANES = 128
BF16_ROWS = 16
VMEM_LIMIT = 56 * 1024 * 1024

SSD_HEADS = 24
SSD_HEAD_DIM = 64
SSD_INNER = SSD_HEADS * SSD_HEAD_DIM
SSD_GROUPS = 4
SSD_STATE = 128
SSD_BC = SSD_GROUPS * SSD_STATE
SSD_CONV_DIM = SSD_INNER + 2 * SSD_BC
SSD_TAPS = 5
CHUNK = 128
POOL_WINDOWS = (2, 4, 8, 16)
POOL_WIDTH = 512
SC_WIDTH = 1024
N_Q_HEADS = 16
N_KV_HEADS = 4
HEAD_DIM = 64
Q_DIM = N_Q_HEADS * HEAD_DIM
KV_DIM = N_KV_HEADS * HEAD_DIM
WINDOW = 128
ROPE_THETA = 10000.0
N_GROUPS = 4
PER_GROUP = 8
N_EXPERTS = N_GROUPS * PER_GROUP
D_EXPERT = 512
MOE_BLOCK = 256
ROW_TILE = 512


def _cparams(n_grid, semantics="parallel"):
    return pltpu.CompilerParams(dimension_semantics=(semantics,) * n_grid, vmem_limit_bytes=VMEM_LIMIT)


def _seq_pos(tok0, ta, la, lb):
    in_a = tok0 < ta
    seqlen = jnp.where(in_a, la, lb)
    pos = jnp.where(in_a, tok0 % la, (tok0 - ta) % lb)
    return pos, seqlen


def _sigmoid(x):
    return 1.0 / (1.0 + jnp.exp(-x))


def _silu(x):
    return x * _sigmoid(x)


def _softplus(x):
    return jnp.maximum(x, 0.0) + jnp.log1p(jnp.exp(-jnp.abs(x)))


def _split3(x):
    hi = x.astype(BF16)
    r1 = x - hi.astype(F32)
    mid = r1.astype(BF16)
    lo = (r1 - mid.astype(F32)).astype(BF16)
    return hi, mid, lo


def _tri_dot(mask, v):
    m = jnp.where(mask, 1.0, 0.0).astype(BF16)
    hi, mid, lo = _split3(v)
    acc = jnp.dot(m, hi, preferred_element_type=F32)
    acc += jnp.dot(m, mid, preferred_element_type=F32)
    acc += jnp.dot(m, lo, preferred_element_type=F32)
    return acc


def _tn_dot(a, b):
    at = a.astype(F32).T.astype(BF16)
    return jnp.dot(at, b, preferred_element_type=F32)


def _norm_proj_kernel(x_ref, nw_ref, w_ref, *out_refs, widths):
    x = x_ref[...]
    ms = jnp.mean(x * x, axis=-1, keepdims=True)
    xn = ((x * lax.rsqrt(ms + RMS_EPS)) * nw_ref[...]).astype(BF16)
    off = 0
    for wd, o in zip(widths, out_refs):
        for c0 in range(0, wd, 512):
            cw = min(512, wd - c0)
            o[:, c0:c0 + cw] = jnp.dot(
                xn, w_ref[:, off + c0:off + c0 + cw], preferred_element_type=F32).astype(o.dtype)
        off += wd


def _norm_proj(x, nw, w, widths, dtypes):
    t, d = x.shape
    n = w.shape[1]
    assert sum(widths) == n and t % ROW_TILE == 0
    return pl.pallas_call(
        functools.partial(_norm_proj_kernel, widths=tuple(widths)),
        grid=(t // ROW_TILE,),
        in_specs=[pl.BlockSpec((ROW_TILE, d), lambda i: (i, 0)),
                  pl.BlockSpec((1, d), lambda i: (0, 0)),
                  pl.BlockSpec((d, n), lambda i: (0, 0))],
        out_specs=[pl.BlockSpec((ROW_TILE, wd), lambda i: (i, 0)) for wd in widths],
        out_shape=[jax.ShapeDtypeStruct((t, wd), dt) for wd, dt in zip(widths, dtypes)],
        compiler_params=_cparams(1),
        name="norm_proj",
    )(x, nw.reshape(1, d), w)


def _halo_specs(r, width, t):
    hb = r // BF16_ROWS
    last = t // BF16_ROWS - 1
    return [pl.BlockSpec((BF16_ROWS, width), lambda i: (jnp.maximum(i * hb - 1, 0), 0)),
            pl.BlockSpec((r, width), lambda i: (i, 0)),
            pl.BlockSpec((BF16_ROWS, width), lambda i: (jnp.minimum((i + 1) * hb, last), 0))]


def _fill_halo_buf(buf, prev_ref, cur_ref, next_ref, r, has_prev, has_next, scale=None):
    def ld(ref):
        v = ref[...].astype(F32)
        return v

    buf[0:BF16_ROWS, :] = jnp.where(has_prev, ld(prev_ref), 0.0)
    buf[BF16_ROWS:BF16_ROWS + r, :] = ld(cur_ref)
    buf[BF16_ROWS + r:2 * BF16_ROWS + r, :] = jnp.where(has_next, ld(next_ref), 0.0)


def _ssd_conv_kernel(prev_ref, cur_ref, next_ref, w_ref, b_ref, xs_ref, bm_ref, cm_ref, buf,
                     *, r, ta, la, lb):
    i = pl.program_id(0)
    pos, seqlen = _seq_pos(i * r, ta, la, lb)
    _fill_halo_buf(buf, prev_ref, cur_ref, next_ref, r, pos > 0, pos + r < seqlen)
    half = SSD_TAPS // 2
    for c0 in range(0, SSD_CONV_DIM, 512):
        for r0 in range(0, r, 128):
            acc = jnp.broadcast_to(b_ref[:, c0:c0 + 512], (128, 512))
            for k in range(SSD_TAPS):
                start = BF16_ROWS + r0 + k - half
                acc = acc + w_ref[k:k + 1, c0:c0 + 512] * buf[start:start + 128, c0:c0 + 512]
            y = _silu(acc).astype(BF16)
            if c0 < SSD_INNER:
                xs_ref[r0:r0 + 128, c0:c0 + 512] = y
            elif c0 < SSD_INNER + SSD_BC:
                bm_ref[r0:r0 + 128, :] = y
            else:
                cm_ref[r0:r0 + 128, :] = y


def _ssd_conv(xbc, conv_w, conv_b, seq):
    t = xbc.shape[0]
    r = ROW_TILE
    return pl.pallas_call(
        functools.partial(_ssd_conv_kernel, r=r, ta=seq[0], la=seq[1], lb=seq[2]),
        grid=(t // r,),
        in_specs=_halo_specs(r, SSD_CONV_DIM, t) + [
            pl.BlockSpec((SSD_TAPS, SSD_CONV_DIM), lambda i: (0, 0)),
            pl.BlockSpec((1, SSD_CONV_DIM), lambda i: (0, 0))],
        out_specs=[pl.BlockSpec((r, SSD_INNER), lambda i: (i, 0)),
                   pl.BlockSpec((r, SSD_BC), lambda i: (i, 0)),
                   pl.BlockSpec((r, SSD_BC), lambda i: (i, 0))],
        out_shape=[jax.ShapeDtypeStruct((t, SSD_INNER), BF16),
                   jax.ShapeDtypeStruct((t, SSD_BC), BF16),
                   jax.ShapeDtypeStruct((t, SSD_BC), BF16)],
        scratch_shapes=[pltpu.VMEM((r + 2 * BF16_ROWS, SSD_CONV_DIM), F32)],
        compiler_params=_cparams(1),
        name="ssd_conv",
    )(xbc, xbc, xbc, conv_w, conv_b.reshape(1, SSD_CONV_DIM))


def _chunk_iotas():
    ri = lax.broadcasted_iota(I32, (CHUNK, CHUNK), 0)
    ci = lax.broadcasted_iota(I32, (CHUNK, CHUNK), 1)
    return ri, ci


def _col(v, lane):
    return jnp.broadcast_to(v[:, lane:lane + 1], (CHUNK, LANES))


def _pair(left, a, b):
    return jnp.where(left, a, b)


def _ssd_bwd_kernel(x_ref, b_ref, c_ref, dt_ref, bias_ref, alog_ref, out_ref, h_ref, xw_ref,
                    *, n_chunks, ta, la, lb):
    c = n_chunks - 1 - pl.program_id(0)
    pos, seqlen = _seq_pos(c * CHUNK, ta, la, lb)

    @pl.when(pos + CHUNK == seqlen)
    def _():
        h_ref[...] = jnp.zeros_like(h_ref)

    ri, ci = _chunk_iotas()
    left = ci < SSD_HEAD_DIM
    dt_all = _softplus(dt_ref[...] + bias_ref[...])
    da = dt_all * (-jnp.exp(alog_ref[...]))
    sfx = _tri_dot(ci >= ri, da)
    wc = dt_all * jnp.exp(sfx[0:1, :] - sfx)
    for g in range(SSD_GROUPS):
        gs = slice(g * 384, (g + 1) * 384)
        cg = c_ref[:, g * SSD_STATE:(g + 1) * SSD_STATE]
        bg = b_ref[:, g * SSD_STATE:(g + 1) * SSD_STATE]
        hg = h_ref[:, gs]
        yoff = jnp.dot(cg, hg.astype(BF16), preferred_element_type=F32)
        cds = []
        for jj in range(3):
            j = g * 3 + jj
            ls = slice(j * LANES, (j + 1) * LANES)
            h0 = SSD_HEADS + 2 * j
            es = jnp.exp(_pair(left, _col(sfx, h0), _col(sfx, h0 + 1)))
            wt = _pair(left, _col(wc, h0), _col(wc, h0 + 1))
            xw_ref[:, ls] = (x_ref[:, ls].astype(F32) * wt).astype(BF16)
            out_ref[:, ls] = (yoff[:, jj * LANES:(jj + 1) * LANES] * es).astype(out_ref.dtype)
            cds.append(es[0:1, :])
        cd = jnp.concatenate(cds, axis=1)
        h_ref[:, gs] = hg * cd + _tn_dot(bg, xw_ref[:, gs])


def _ssd_main_kernel(x_ref, b_ref, c_ref, z_ref, dt_ref, ybo_ref, bias_ref, alog_ref, dskip_ref,
                     gw_ref, out_ref, h_ref, xw_ref, y_ref, rows_ref, *, ta, la, lb):
    pos, _ = _seq_pos(pl.program_id(0) * CHUNK, ta, la, lb)

    @pl.when(pos == 0)
    def _():
        h_ref[...] = jnp.zeros_like(h_ref)

    ri, ci = _chunk_iotas()
    left = ci < SSD_HEAD_DIM
    lower = ci <= ri
    low_s = ci < ri
    up_s = ci > ri
    dt_all = _softplus(dt_ref[...] + bias_ref[...])
    da = dt_all * (-jnp.exp(alog_ref[...]))
    acs = _tri_dot(lower, da)
    sfx = _tri_dot(ci >= ri, da)
    wc = dt_all * jnp.exp(acs[CHUNK - 1:CHUNK, :] - acs)
    rows_ref[0] = acs.T
    rows_ref[1] = sfx.T
    rows_ref[2] = dt_all.T

    def row(k, h):
        return jnp.broadcast_to(rows_ref[k, h:h + 1, :], (CHUNK, CHUNK))

    for g in range(SSD_GROUPS):
        gs = slice(g * 384, (g + 1) * 384)
        cg = c_ref[:, g * SSD_STATE:(g + 1) * SSD_STATE]
        bg = b_ref[:, g * SSD_STATE:(g + 1) * SSD_STATE]
        cb = lax.dot_general(cg, bg, (((1,), (1,)), ((), ())), preferred_element_type=F32)
        hg = h_ref[:, gs]
        yoff = jnp.dot(cg, hg.astype(BF16), preferred_element_type=F32)
        cds = []
        for jj in range(3):
            j = g * 3 + jj
            ls = slice(j * LANES, (j + 1) * LANES)
            xt = x_ref[:, ls]
            xf = xt.astype(F32)
            yd = jnp.zeros((CHUNK, LANES), F32)
            cols = []
            for s in range(2):
                h = 2 * j + s
                hb = SSD_HEADS + h
                colf = _col(acs, h)
                cols.append(colf)
                arg = jnp.where(lower, colf - row(0, h), _col(sfx, hb) - row(1, hb))
                dtf = row(2, h)
                dtb = row(2, hb)
                dsel = jnp.where(low_s, dtf, jnp.where(up_s, dtb, dtf + dtb))
                m = (cb * jnp.exp(arg) * dsel).astype(BF16)
                xm = jnp.where(left if s == 0 else jnp.logical_not(left), xf, 0.0).astype(BF16)
                yd = yd + jnp.dot(m, xm, preferred_element_type=F32)
            es = jnp.exp(_pair(left, cols[0], cols[1]))
            wt = _pair(left, _col(wc, 2 * j), _col(wc, 2 * j + 1))
            xw_ref[:, ls] = (xf * wt).astype(BF16)
            y_ref[:, ls] = (yd + yoff[:, jj * LANES:(jj + 1) * LANES] * es
                            + ybo_ref[:, ls].astype(F32) + xf * dskip_ref[:, ls])
            cds.append(es[CHUNK - 1:CHUNK, :])
        cd = jnp.concatenate(cds, axis=1)
        h_ref[:, gs] = hg * cd + _tn_dot(bg, xw_ref[:, gs])

    y = y_ref[...] * _silu(z_ref[...].astype(F32))
    ms = jnp.mean(y * y, axis=-1, keepdims=True)
    out_ref[...] = ((y * lax.rsqrt(ms + RMS_EPS)) * gw_ref[...]).astype(out_ref.dtype)


def _ssd_scan(xs, bm, cm, z, dt, dt_bias, a_log, d_skip, gnorm_w, seq):
    t = xs.shape[0]
    n = t // CHUNK
    ta, la, lb = seq
    pad = LANES - 2 * SSD_HEADS
    bias = jnp.pad(dt_bias.reshape(1, 2 * SSD_HEADS), ((0, 0), (0, pad)))
    alog = jnp.pad(a_log.reshape(1, 2 * SSD_HEADS), ((0, 0), (0, pad)))
    dexp = jnp.repeat(d_skip, SSD_HEAD_DIM).reshape(1, SSD_INNER)
    small = lambda w: pl.BlockSpec((1, w), lambda i: (0, 0))

    rev = lambda i: (n - 1 - i, 0)
    ybo = pl.pallas_call(
        functools.partial(_ssd_bwd_kernel, n_chunks=n, ta=ta, la=la, lb=lb),
        grid=(n,),
        in_specs=[pl.BlockSpec((CHUNK, SSD_INNER), rev),
                  pl.BlockSpec((CHUNK, SSD_BC), rev),
                  pl.BlockSpec((CHUNK, SSD_BC), rev),
                  pl.BlockSpec((CHUNK, LANES), rev),
                  small(LANES), small(LANES)],
        out_specs=pl.BlockSpec((CHUNK, SSD_INNER), rev),
        out_shape=jax.ShapeDtypeStruct((t, SSD_INNER), BF16),
        scratch_shapes=[pltpu.VMEM((SSD_STATE, SSD_INNER), F32),
                        pltpu.VMEM((CHUNK, SSD_INNER), BF16)],
        compiler_params=_cparams(1, "arbitrary"),
        name="ssd_bwd",
    )(xs, bm, cm, dt, bias, alog)

    fwd = lambda i: (i, 0)
    return pl.pallas_call(
        functools.partial(_ssd_main_kernel, ta=ta, la=la, lb=lb),
        grid=(n,),
        in_specs=[pl.BlockSpec((CHUNK, SSD_INNER), fwd),
                  pl.BlockSpec((CHUNK, SSD_BC), fwd),
                  pl.BlockSpec((CHUNK, SSD_BC), fwd),
                  pl.BlockSpec((CHUNK, SSD_INNER), fwd),
                  pl.BlockSpec((CHUNK, LANES), fwd),
                  pl.BlockSpec((CHUNK, SSD_INNER), fwd),
                  small(LANES), small(LANES), small(SSD_INNER), small(SSD_INNER)],
        out_specs=pl.BlockSpec((CHUNK, SSD_INNER), fwd),
        out_shape=jax.ShapeDtypeStruct((t, SSD_INNER), BF16),
        scratch_shapes=[pltpu.VMEM((SSD_STATE, SSD_INNER), F32),
                        pltpu.VMEM((CHUNK, SSD_INNER), BF16),
                        pltpu.VMEM((CHUNK, SSD_INNER), F32),
                        pltpu.VMEM((3, CHUNK, CHUNK), F32)],
        compiler_params=_cparams(1, "arbitrary"),
        name="ssd_main",
    )(xs, bm, cm, z, dt, ybo, bias, alog, dexp, gnorm_w.reshape(1, SSD_INNER))


def _pool_kernel(prev_ref, cur_ref, next_ref, w_ref, sc_ref, out_ref, buf, *, r, ta, la, lb):
    i = pl.program_id(0)
    pos, seqlen = _seq_pos(i * r, ta, la, lb)
    _fill_halo_buf(buf, prev_ref, cur_ref, next_ref, r, pos > 0, pos + r < seqlen)
    for gi, w in enumerate(POOL_WINDOWS):
        ls = slice(gi * LANES, (gi + 1) * LANES)
        for r0 in range(0, r, 128):
            base = BF16_ROWS + r0
            acc = buf[base - w // 2:base - w // 2 + 128, ls]
            for k in range(1, w):
                s = base - w // 2 + k
                acc = acc + buf[s:s + 128, ls]
            tpos = pos + r0 + lax.broadcasted_iota(I32, (128, LANES), 0)
            lo = jnp.maximum(tpos - w // 2, 0)
            hi = jnp.minimum(tpos - w // 2 + w, seqlen)
            mean = acc / (hi - lo).astype(F32)
            diff = (mean - buf[base:base + 128, ls]).astype(BF16)
            o = jnp.dot(diff, w_ref[gi], preferred_element_type=F32) * sc_ref[:, ls]
            out_ref[r0:r0 + 128, ls] = o.astype(out_ref.dtype)


def _pool_mixer(xp, pool_w, pool_scale, seq):
    t = xp.shape[0]
    r = ROW_TILE
    return pl.pallas_call(
        functools.partial(_pool_kernel, r=r, ta=seq[0], la=seq[1], lb=seq[2]),
        grid=(t // r,),
        in_specs=_halo_specs(r, POOL_WIDTH, t) + [
            pl.BlockSpec((len(POOL_WINDOWS), LANES, LANES), lambda i: (0, 0, 0)),
            pl.BlockSpec((1, POOL_WIDTH), lambda i: (0, 0))],
        out_specs=pl.BlockSpec((r, POOL_WIDTH), lambda i: (i, 0)),
        out_shape=jax.ShapeDtypeStruct((t, POOL_WIDTH), BF16),
        scratch_shapes=[pltpu.VMEM((r + 2 * BF16_ROWS, POOL_WIDTH), F32)],
        compiler_params=_cparams(1),
        name="pool_mixer",
    )(xp, xp, xp, pool_w.astype(BF16), pool_scale.reshape(1, POOL_WIDTH))


def _short_conv_kernel(gcp, gcc, gcn, xp, xc, xn, gb_ref, w_ref, out_ref, buf, *, r, ta, la, lb):
    i = pl.program_id(0)
    pos, seqlen = _seq_pos(i * r, ta, la, lb)
    has_prev = pos > 0
    has_next = pos + r < seqlen
    prod = lambda a, b: a[...].astype(F32) * b[...].astype(F32)
    buf[0:BF16_ROWS, :] = jnp.where(has_prev, prod(gcp, xp), 0.0)
    buf[BF16_ROWS:BF16_ROWS + r, :] = prod(gcc, xc)
    buf[BF16_ROWS + r:2 * BF16_ROWS + r, :] = jnp.where(has_next, prod(gcn, xn), 0.0)
    for c0 in range(0, SC_WIDTH, 512):
        cs = slice(c0, c0 + 512)
        for r0 in range(0, r, 128):
            base = BF16_ROWS + r0
            acc = w_ref[0:1, cs] * buf[base - 1:base + 127, cs]
            acc = acc + w_ref[1:2, cs] * buf[base:base + 128, cs]
            acc = acc + w_ref[2:3, cs] * buf[base + 1:base + 129, cs]
            out_ref[r0:r0 + 128, cs] = (gb_ref[r0:r0 + 128, cs].astype(F32) * acc).astype(out_ref.dtype)


def _short_conv(gb, gc, xc, conv_w, seq):
    t = gb.shape[0]
    r = ROW_TILE
    halo = _halo_specs(r, SC_WIDTH, t)
    return pl.pallas_call(
        functools.partial(_short_conv_kernel, r=r, ta=seq[0], la=seq[1], lb=seq[2]),
        grid=(t // r,),
        in_specs=halo + halo + [pl.BlockSpec((r, SC_WIDTH), lambda i: (i, 0)),
                                pl.BlockSpec((3, SC_WIDTH), lambda i: (0, 0))],
        out_specs=pl.BlockSpec((r, SC_WIDTH), lambda i: (i, 0)),
        out_shape=jax.ShapeDtypeStruct((t, SC_WIDTH), BF16),
        scratch_shapes=[pltpu.VMEM((r + 2 * BF16_ROWS, SC_WIDTH), F32)],
        compiler_params=_cparams(1),
        name="short_conv",
    )(gc, gc, gc, xc, xc, xc, gb, conv_w)


def _rope(t, cos, sin_signed):
    n = t.shape[1]
    lane = lax.broadcasted_iota(I32, t.shape, 1)
    first = (lane % HEAD_DIM) < HEAD_DIM // 2
    rot = jnp.where(first, pltpu.roll(t, n - HEAD_DIM // 2, 1), pltpu.roll(t, HEAD_DIM // 2, 1))
    reps = n // LANES
    return t * jnp.tile(cos, (1, reps)) + rot * jnp.tile(sin_signed, (1, reps))


def _attn_kernel(q_ref, kp_ref, kc_ref, kn_ref, vp_ref, vc_ref, vn_ref, cp_ref, cc_ref, cn_ref,
                 sp_ref, sc_ref, sn_ref, sink_ref, out_ref, *, ta, la, lb):
    pos, seqlen = _seq_pos(pl.program_id(0) * CHUNK, ta, la, lb)
    has_prev = pos > 0
    has_next = pos + CHUNK < seqlen
    q = _rope(q_ref[...].astype(F32), cc_ref[...], sc_ref[...]).astype(BF16)
    ks = [_rope(k[...].astype(F32), c[...], s[...]).astype(BF16)
          for k, c, s in ((kp_ref, cp_ref, sp_ref), (kc_ref, cc_ref, sc_ref), (kn_ref, cn_ref, sn_ref))]
    kb = jnp.concatenate(ks, axis=0)
    vb = jnp.concatenate([vp_ref[...], vc_ref[...], vn_ref[...]], axis=0)
    rows = 4 * CHUNK
    qi = lax.broadcasted_iota(I32, (rows, 3 * CHUNK), 0) % CHUNK
    kk = lax.broadcasted_iota(I32, (rows, 3 * CHUNK), 1)
    mask = (kk >= qi) & (kk <= qi + 2 * WINDOW)
    mask = mask & ((kk >= CHUNK) | has_prev) & ((kk < 2 * CHUNK) | has_next)
    scale = HEAD_DIM ** -0.5
    for g in range(N_KV_HEADS):
        kg = kb[:, g * HEAD_DIM:(g + 1) * HEAD_DIM]
        vg = vb[:, g * HEAD_DIM:(g + 1) * HEAD_DIM]
        qg = jnp.concatenate([q[:, (4 * g + rr) * HEAD_DIM:(4 * g + rr + 1) * HEAD_DIM]
                              for rr in range(4)], axis=0)
        s = lax.dot_general(qg, kg, (((1,), (1,)), ((), ())), preferred_element_type=F32) * scale
        s = jnp.where(mask, s, -jnp.inf)
        sink = jnp.concatenate([jnp.broadcast_to(sink_ref[:, 4 * g + rr:4 * g + rr + 1], (CHUNK, 1))
                                for rr in range(4)], axis=0)
        m = jnp.maximum(jnp.max(s, axis=-1, keepdims=True), sink)
        p = jnp.exp(s - m)
        den = jnp.sum(p, axis=-1, keepdims=True) + jnp.exp(sink - m)
        p = (p / den).astype(BF16)
        o = jnp.dot(p, vg, preferred_element_type=F32)
        for rr in range(4):
            hq = 4 * g + rr
            out_ref[:, hq * HEAD_DIM:(hq + 1) * HEAD_DIM] = o[rr * CHUNK:(rr + 1) * CHUNK].astype(out_ref.dtype)


def _rope_tables(lmax):
    inv = 1.0 / (ROPE_THETA ** (jnp.arange(0, HEAD_DIM, 2, dtype=F32) / HEAD_DIM))
    ang = jnp.arange(lmax, dtype=F32)[:, None] * inv[None, :]
    cos, sin = jnp.cos(ang), jnp.sin(ang)
    cos = jnp.concatenate([cos, cos, cos, cos], axis=1)
    sin = jnp.concatenate([-sin, sin, -sin, sin], axis=1)
    return cos, sin


def _attention(q, k, v, sinks, seq):
    t = q.shape[0]
    n = t // CHUNK
    ta, la, lb = seq
    cos, sin = _rope_tables(max(la, lb))
    nlast = n - 1
    tlast = max(la, lb) // CHUNK - 1

    def pblk(i):
        pos, _ = _seq_pos(i * CHUNK, ta, la, lb)
        return pos // CHUNK

    cur = lambda i: (i, 0)
    prv = lambda i: (jnp.maximum(i - 1, 0), 0)
    nxt = lambda i: (jnp.minimum(i + 1, nlast), 0)
    tcur = lambda i: (pblk(i), 0)
    tprv = lambda i: (jnp.maximum(pblk(i) - 1, 0), 0)
    tnxt = lambda i: (jnp.minimum(pblk(i) + 1, tlast), 0)
    kv = lambda f: pl.BlockSpec((CHUNK, KV_DIM), f)
    tb = lambda f: pl.BlockSpec((CHUNK, LANES), f)
    return pl.pallas_call(
        functools.partial(_attn_kernel, ta=ta, la=la, lb=lb),
        grid=(n,),
        in_specs=[pl.BlockSpec((CHUNK, Q_DIM), cur), kv(prv), kv(cur), kv(nxt), kv(prv), kv(cur), kv(nxt),
                  tb(tprv), tb(tcur), tb(tnxt), tb(tprv), tb(tcur), tb(tnxt),
                  pl.BlockSpec((1, N_Q_HEADS), lambda i: (0, 0))],
        out_specs=pl.BlockSpec((CHUNK, Q_DIM), cur),
        out_shape=jax.ShapeDtypeStruct((t, Q_DIM), BF16),
        compiler_params=_cparams(1),
        name="band_attention",
    )(q, k, k, k, v, v, v, cos, cos, cos, sin, sin, sin, sinks.reshape(1, N_Q_HEADS))


def _out_router_kernel(a_ref, b_ref, h_ref, wa_ref, wb_ref, nw_ref, wr_ref, br_ref,
                       h_out, eidx_out, wts_out):
    h1 = (h_ref[...] + jnp.dot(a_ref[...], wa_ref[...], preferred_element_type=F32)
          + jnp.dot(b_ref[...], wb_ref[...], preferred_element_type=F32))
    h_out[...] = h1
    ms = jnp.mean(h1 * h1, axis=-1, keepdims=True)
    tn = (h1 * lax.rsqrt(ms + RMS_EPS)) * nw_ref[...]
    t_hi, t_mid, _ = _split3(tn)
    w_hi, w_mid, _ = _split3(wr_ref[...])
    nt = (((1,), (1,)), ((), ()))
    logits = (lax.dot_general(w_hi, t_hi, nt, preferred_element_type=F32)
              + lax.dot_general(w_hi, t_mid, nt, preferred_element_type=F32)
              + lax.dot_general(w_mid, t_hi, nt, preferred_element_type=F32)) + br_ref[...]
    tm = logits.shape[1]
    row = lax.broadcasted_iota(I32, (PER_GROUP, tm), 0)
    gl = jnp.where(row < N_GROUPS, logits[0:PER_GROUP], -jnp.inf)
    gmax = jnp.max(gl, axis=0, keepdims=True)
    gsel = jnp.min(jnp.where(gl == gmax, row, PER_GROUP), axis=0, keepdims=True)
    p_group = 1.0 / jnp.sum(jnp.exp(gl - gmax), axis=0, keepdims=True)
    el = logits[PER_GROUP:2 * PER_GROUP]
    for gi in range(1, N_GROUPS):
        el = jnp.where(gsel == gi, logits[(gi + 1) * PER_GROUP:(gi + 2) * PER_GROUP], el)
    m1 = jnp.max(el, axis=0, keepdims=True)
    i1 = jnp.min(jnp.where(el == m1, row, PER_GROUP), axis=0, keepdims=True)
    el2 = jnp.where(row == i1, -jnp.inf, el)
    m2 = jnp.max(el2, axis=0, keepdims=True)
    i2 = jnp.min(jnp.where(el2 == m2, row, PER_GROUP), axis=0, keepdims=True)
    e2 = jnp.exp(m2 - m1)
    w1 = p_group / (1.0 + e2)
    w2 = p_group * e2 / (1.0 + e2)
    eidx_out[...] = jnp.where(row == 0, gsel * PER_GROUP + i1, jnp.where(row == 1, gsel * PER_GROUP + i2, 0))
    wrow = lax.broadcasted_iota(I32, (LANES, tm), 0)
    wmat = jnp.where(wrow == 0, w1, jnp.where(wrow == 1, w2, 0.0))
    wts_out[...] = wmat.T


def _out_router(a, b, h, w_out, nw, w_group, b_group, w_expert, b_expert):
    t, d = h.shape
    ka, kb = a.shape[1], b.shape[1]
    wa = w_out[:ka].astype(BF16)
    wb = w_out[ka:].astype(BF16)
    wr = jnp.zeros((LANES, d), F32).at[0:N_GROUPS].set(w_group.T).at[PER_GROUP:PER_GROUP + N_EXPERTS].set(w_expert.T)
    br = jnp.zeros((LANES, 1), F32).at[0:N_GROUPS, 0].set(b_group).at[PER_GROUP:PER_GROUP + N_EXPERTS, 0].set(b_expert)
    r = ROW_TILE
    row_spec = lambda w: pl.BlockSpec((r, w), lambda i: (i, 0))
    const = lambda s: pl.BlockSpec(s, lambda i: (0, 0))
    return pl.pallas_call(
        _out_router_kernel,
        grid=(t // r,),
        in_specs=[row_spec(ka), row_spec(kb), row_spec(d), const((ka, d)), const((kb, d)), const((1, d)),
                  const((LANES, d)), const((LANES, 1))],
        out_specs=[row_spec(d), pl.BlockSpec((PER_GROUP, r), lambda i: (0, i)), row_spec(LANES)],
        out_shape=[jax.ShapeDtypeStruct((t, d), F32), jax.ShapeDtypeStruct((PER_GROUP, t), I32),
                   jax.ShapeDtypeStruct((t, LANES), F32)],
        compiler_params=_cparams(1),
        name="out_proj_router",
    )(a, b, h, wa, wb, nw.reshape(1, d), wr, br)


def _rank_kernel(eidx_ref, rank_out, count_out, carry):
    @pl.when(pl.program_id(0) == 0)
    def _():
        carry[...] = jnp.zeros_like(carry)

    tm = eidx_ref.shape[1]
    erow = lax.broadcasted_iota(I32, (N_EXPERTS, tm), 0)
    ti = lax.broadcasted_iota(I32, (tm, tm), 0)
    tj = lax.broadcasted_iota(I32, (tm, tm), 1)
    before = jnp.where(ti < tj, 1.0, 0.0).astype(BF16)
    base = carry[:, 0:1]
    ranks = []
    for k in range(2):
        oh = jnp.where(eidx_ref[k:k + 1, :] == erow, 1.0, 0.0)
        prefix = jnp.dot(oh.astype(BF16), before, preferred_element_type=F32)
        ranks.append(jnp.sum(oh * (prefix + base), axis=0, keepdims=True))
        base = base + jnp.sum(oh, axis=1, keepdims=True)
    carry[...] = jnp.broadcast_to(base, carry.shape)
    row = lax.broadcasted_iota(I32, (PER_GROUP, tm), 0)
    rank_out[...] = jnp.where(row == 0, ranks[0], jnp.where(row == 1, ranks[1], 0.0)).astype(I32)
    count_out[...] = carry[...].astype(I32)


def _rank(eidx):
    t = eidx.shape[1]
    r = ROW_TILE
    return pl.pallas_call(
        _rank_kernel,
        grid=(t // r,),
        in_specs=[pl.BlockSpec((PER_GROUP, r), lambda i: (0, i))],
        out_specs=[pl.BlockSpec((PER_GROUP, r), lambda i: (0, i)),
                   pl.BlockSpec((N_EXPERTS, LANES), lambda i: (0, 0))],
        out_shape=[jax.ShapeDtypeStruct((PER_GROUP, t), I32), jax.ShapeDtypeStruct((N_EXPERTS, LANES), I32)],
        scratch_shapes=[pltpu.VMEM((N_EXPERTS, LANES), F32)],
        compiler_params=_cparams(1, "arbitrary"),
        name="moe_rank",
    )(eidx)


def _dispatch_kernel(dest_ref, x_ref, init_ref, out_ref, sem):
    del init_ref
    tm = x_ref.shape[0]

    def copy(r, k):
        return pltpu.make_async_copy(x_ref.at[pl.ds(r, 1)], out_ref.at[pl.ds(dest_ref[0, k, r], 1)], sem)

    def start(r, c):
        copy(r, 0).start()
        copy(r, 1).start()
        return c

    def wait(r, c):
        copy(r, 0).wait()
        copy(r, 1).wait()
        return c

    lax.fori_loop(0, tm, start, 0)
    lax.fori_loop(0, tm, wait, 0)


def _dispatch(dest3, h, n_slots):
    t, d = h.shape
    r = ROW_TILE
    init = jnp.zeros((n_slots, d), F32)
    return pl.pallas_call(
        _dispatch_kernel,
        grid=(t // r,),
        in_specs=[pl.BlockSpec((1, 2, r), lambda i: (i, 0, 0), memory_space=pltpu.SMEM),
                  pl.BlockSpec((r, d), lambda i: (i, 0)),
                  pl.BlockSpec(memory_space=pl.ANY)],
        out_specs=pl.BlockSpec(memory_space=pl.ANY),
        out_shape=jax.ShapeDtypeStruct((n_slots, d), F32),
        scratch_shapes=[pltpu.SemaphoreType.DMA],
        input_output_aliases={2: 0},
        compiler_params=_cparams(1, "arbitrary"),
        name="moe_dispatch",
    )(dest3, h, init)


def _expert_kernel(be_ref, nb_ref, x_ref, nw_ref, w1_ref, w3_ref, w2_ref, out_ref):
    del be_ref
    i = pl.program_id(0)

    @pl.when(i < nb_ref[0])
    def _():
        x = x_ref[...]
        ms = jnp.mean(x * x, axis=-1, keepdims=True)
        xn = ((x * lax.rsqrt(ms + RMS_EPS)) * nw_ref[...]).astype(BF16)
        a = jnp.dot(xn, w1_ref[0], preferred_element_type=F32)
        b = jnp.dot(xn, w3_ref[0], preferred_element_type=F32)
        mid = (_silu(a) * b).astype(BF16)
        out_ref[...] = jnp.dot(mid, w2_ref[0], preferred_element_type=F32)

    @pl.when(i >= nb_ref[0])
    def _():
        out_ref[...] = jnp.zeros_like(out_ref)


def _experts(xin, block_e, n_used, nw, w1, w3, w2):
    n_slots, d = xin.shape
    nb = n_slots // MOE_BLOCK
    f = w1.shape[2]
    grid_spec = pltpu.PrefetchScalarGridSpec(
        num_scalar_prefetch=2,
        grid=(nb,),
        in_specs=[pl.BlockSpec((MOE_BLOCK, d), lambda i, be, nu: (i, 0)),
                  pl.BlockSpec((1, d), lambda i, be, nu: (0, 0)),
                  pl.BlockSpec((1, d, f), lambda i, be, nu: (be[i], 0, 0)),
                  pl.BlockSpec((1, d, f), lambda i, be, nu: (be[i], 0, 0)),
                  pl.BlockSpec((1, f, d), lambda i, be, nu: (be[i], 0, 0))],
        out_specs=pl.BlockSpec((MOE_BLOCK, d), lambda i, be, nu: (i, 0)),
    )
    return pl.pallas_call(
        _expert_kernel,
        grid_spec=grid_spec,
        out_shape=jax.ShapeDtypeStruct((n_slots, d), F32),
        compiler_params=_cparams(1, "arbitrary"),
        name="moe_experts",
    )(block_e, n_used, xin, nw.reshape(1, d), w1, w3, w2)


def _combine_kernel(dest_ref, h_ref, wts_ref, y_ref, nw_ref, out_ref, gbuf, sem, *, final_norm):
    tm = h_ref.shape[0]

    def copy(r, k):
        return pltpu.make_async_copy(y_ref.at[pl.ds(dest_ref[0, k, r], 1)], gbuf.at[k, pl.ds(r, 1)], sem)

    def start(r, c):
        copy(r, 0).start()
        copy(r, 1).start()
        return c

    def wait(r, c):
        copy(r, 0).wait()
        copy(r, 1).wait()
        return c

    lax.fori_loop(0, tm, start, 0)
    lax.fori_loop(0, tm, wait, 0)
    wts = wts_ref[...]
    out = h_ref[...] + (wts[:, 0:1] * gbuf[0] + wts[:, 1:2] * gbuf[1])
    if final_norm:
        ms = jnp.mean(out * out, axis=-1, keepdims=True)
        out = (out * lax.rsqrt(ms + RMS_EPS)) * nw_ref[...]
    out_ref[...] = out


def _combine(dest3, h, wts, yout, nw, final_norm):
    t, d = h.shape
    r = ROW_TILE
    return pl.pallas_call(
        functools.partial(_combine_kernel, final_norm=final_norm),
        grid=(t // r,),
        in_specs=[pl.BlockSpec((1, 2, r), lambda i: (i, 0, 0), memory_space=pltpu.SMEM),
                  pl.BlockSpec((r, d), lambda i: (i, 0)),
                  pl.BlockSpec((r, LANES), lambda i: (i, 0)),
                  pl.BlockSpec(memory_space=pl.ANY),
                  pl.BlockSpec((1, d), lambda i: (0, 0))],
        out_specs=pl.BlockSpec((r, d), lambda i: (i, 0)),
        out_shape=jax.ShapeDtypeStruct((t, d), F32),
        scratch_shapes=[pltpu.VMEM((2, r, d), F32), pltpu.SemaphoreType.DMA],
        compiler_params=_cparams(1, "arbitrary"),
        name="moe_combine",
    )(dest3, h, wts, yout, nw.reshape(1, d))


def _moe(h1, eidx, wts, norm_w, w1, w3, w2, final_nw):
    t, d = h1.shape
    r = ROW_TILE
    rank, counts = _rank(eidx)
    counts = counts[:, 0]
    padded = (counts + MOE_BLOCK - 1) // MOE_BLOCK * MOE_BLOCK
    pad_end = jnp.cumsum(padded)
    pad_start = pad_end - padded
    n_blocks = -(-(2 * t) // MOE_BLOCK) + N_EXPERTS
    n_used = (pad_end[-1] // MOE_BLOCK).astype(I32).reshape(1)
    blk = jnp.minimum(jnp.arange(n_blocks, dtype=I32), n_used[0] - 1) * MOE_BLOCK
    block_e = jnp.minimum(jnp.searchsorted(pad_end, blk, side="right"), N_EXPERTS - 1).astype(I32)
    dest = pad_start.astype(I32)[eidx[0:2]] + rank[0:2]
    dest3 = dest.reshape(2, t // r, r).transpose(1, 0, 2)
    xin = _dispatch(dest3, h1, n_blocks * MOE_BLOCK)
    yout = _experts(xin, block_e, n_used, norm_w, w1.astype(BF16), w3.astype(BF16), w2.astype(BF16))
    nw = final_nw if final_nw is not None else norm_w
    return _combine(dest3, h1, wts, yout, nw, final_nw is not None)


def _trunk(x, seq, p):
    wi = p["even_w_in"][0]
    z_end = SSD_INNER
    xbc_end = z_end + SSD_CONV_DIM
    dt_end = xbc_end + 2 * SSD_HEADS
    w0 = jnp.concatenate([wi[:, :xbc_end], wi[:, dt_end:], wi[:, xbc_end:dt_end],
                          jnp.zeros((D_MODEL, LANES - 2 * SSD_HEADS), F32)], axis=1).astype(BF16)
    z, xbc, xp, dt = _norm_proj(x, p["norm_mix"][0], w0, (SSD_INNER, SSD_CONV_DIM, POOL_WIDTH, LANES),
                                (BF16, BF16, BF16, F32))
    xs, bm, cm = _ssd_conv(xbc, p["ssd_conv_w"][0], p["ssd_conv_b"][0], seq)
    y_ssd = _ssd_scan(xs, bm, cm, z, dt, p["ssd_dt_bias"][0], p["ssd_A_log"][0], p["ssd_D"][0],
                      p["ssd_norm_w"][0], seq)
    y_pool = _pool_mixer(xp, p["pool_w"][0], p["pool_scale"][0], seq)
    h, eidx, wts = _out_router(y_ssd, y_pool, x, p["even_w_out"][0], p["norm_ffn"][0], p["moe_w_group"][0],
                               p["moe_b_group"][0], p["moe_w_expert"][0], p["moe_b_expert"][0])
    h = _moe(h, eidx, wts, p["norm_ffn"][0], p["moe_w1"][0], p["moe_w3"][0], p["moe_w2"][0], None)
    gb, gc, xc, q, k, v = _norm_proj(h, p["norm_mix"][1], p["odd_w_in"][0].astype(BF16),
                                     (SC_WIDTH, SC_WIDTH, SC_WIDTH, Q_DIM, KV_DIM, KV_DIM), (BF16,) * 6)
    y_conv = _short_conv(gb, gc, xc, p["sc_conv_w"][0], seq)
    y_attn = _attention(q, k, v, p["attn_sinks"][0], seq)
    h, eidx, wts = _out_router(y_conv, y_attn, h, p["odd_w_out"][0], p["norm_ffn"][1], p["moe_w_group"][1],
                               p["moe_b_group"][1], p["moe_w_expert"][1], p["moe_b_expert"][1])
    return _moe(h, eidx, wts, p["norm_ffn"][1], p["moe_w1"][1], p["moe_w3"][1], p["moe_w2"][1], p["norm_final"])


def kernel(x_prompt, x_sample, norm_mix, norm_ffn, norm_final, even_w_in, ssd_conv_w, ssd_conv_b, ssd_A_log,
           ssd_dt_bias, ssd_D, ssd_norm_w, pool_w, pool_scale, even_w_out, odd_w_in, sc_conv_w, attn_sinks,
           odd_w_out, moe_w_group, moe_b_group, moe_w_expert, moe_b_expert, moe_w1, moe_w3, moe_w2):
    p = dict(norm_mix=norm_mix, norm_ffn=norm_ffn, norm_final=norm_final, even_w_in=even_w_in,
             ssd_conv_w=ssd_conv_w, ssd_conv_b=ssd_conv_b, ssd_A_log=ssd_A_log, ssd_dt_bias=ssd_dt_bias,
             ssd_D=ssd_D, ssd_norm_w=ssd_norm_w, pool_w=pool_w, pool_scale=pool_scale, even_w_out=even_w_out,
             odd_w_in=odd_w_in, sc_conv_w=sc_conv_w, attn_sinks=attn_sinks, odd_w_out=odd_w_out,
             moe_w_group=moe_w_group, moe_b_group=moe_b_group, moe_w_expert=moe_w_expert,
             moe_b_expert=moe_b_expert, moe_w1=moe_w1, moe_w3=moe_w3, moe_w2=moe_w2)
    bp, lp, d = x_prompt.shape
    bs, ls, _ = x_sample.shape
    ta = bp * lp
    x = jnp.concatenate([x_prompt.reshape(ta, d), x_sample.reshape(bs * ls, d)], axis=0)
    y = _trunk(x, (ta, lp, ls), p)
    return y[:ta].reshape(bp, lp, d), y[ta:].reshape(bs, ls, d)
```

```python
import functools

import jax
import jax.numpy as jnp
from jax import lax
from jax.experimental import pallas as pl
from jax.experimental.pallas import tpu as pltpu

F32 = jnp.float32
BF16 = jnp.bfloat16
I32 = jnp.int32

RMS_EPS = 1e-6
D_MODEL = 1024
LANES = 128
BF16_ROWS = 16
VMEM_LIMIT = 56 * 1024 * 1024

SSD_HEADS = 24
SSD_HEAD_DIM = 64
SSD_INNER = SSD_HEADS * SSD_HEAD_DIM
SSD_GROUPS = 4
SSD_STATE = 128
SSD_BC = SSD_GROUPS * SSD_STATE
SSD_CONV_DIM = SSD_INNER + 2 * SSD_BC
SSD_TAPS = 5
CHUNK = 128
POOL_WINDOWS = (2, 4, 8, 16)
POOL_WIDTH = 512
SC_WIDTH = 1024
N_Q_HEADS = 16
N_KV_HEADS = 4
HEAD_DIM = 64
Q_DIM = N_Q_HEADS * HEAD_DIM
KV_DIM = N_KV_HEADS * HEAD_DIM
WINDOW = 128
ROPE_THETA = 10000.0
N_GROUPS = 4
PER_GROUP = 8
N_EXPERTS = N_GROUPS * PER_GROUP
D_EXPERT = 512
MOE_BLOCK = 256
ROW_TILE = 512
DMA_UNROLL = 8


def _cparams(n_grid, semantics="parallel"):
    return pltpu.CompilerParams(dimension_semantics=(semantics,) * n_grid, vmem_limit_bytes=VMEM_LIMIT)


def _seq_pos(tok0, ta, la, lb):
    in_a = tok0 < ta
    seqlen = jnp.where(in_a, la, lb)
    pos = jnp.where(in_a, tok0 % la, (tok0 - ta) % lb)
    return pos, seqlen


def _sigmoid(x):
    return 1.0 / (1.0 + jnp.exp(-x))


def _silu(x):
    return x * _sigmoid(x)


def _softplus(x):
    return jnp.maximum(x, 0.0) + jnp.log1p(jnp.exp(-jnp.abs(x)))


def _split3(x):
    hi = x.astype(BF16)
    r1 = x - hi.astype(F32)
    mid = r1.astype(BF16)
    lo = (r1 - mid.astype(F32)).astype(BF16)
    return hi, mid, lo


def _tri_dot(mask, v):
    m = jnp.where(mask, 1.0, 0.0).astype(BF16)
    hi, mid, lo = _split3(v)
    acc = jnp.dot(m, hi, preferred_element_type=F32)
    acc += jnp.dot(m, mid, preferred_element_type=F32)
    acc += jnp.dot(m, lo, preferred_element_type=F32)
    return acc


def _tn_dot(a, b):
    at = a.astype(F32).T.astype(BF16)
    return jnp.dot(at, b, preferred_element_type=F32)


def _rope(t, cos, sin_signed):
    n = t.shape[1]
    lane = lax.broadcasted_iota(I32, t.shape, 1)
    first = (lane % HEAD_DIM) < HEAD_DIM // 2
    rot = jnp.where(first, pltpu.roll(t, n - HEAD_DIM // 2, 1), pltpu.roll(t, HEAD_DIM // 2, 1))
    reps = n // LANES
    return t * jnp.tile(cos, (1, reps)) + rot * jnp.tile(sin_signed, (1, reps))


def _norm_proj_kernel(x_ref, nw_ref, w_ref, *refs, widths, rope_scales):
    use_rope = any(s is not None for s in rope_scales)
    if use_rope:
        cos_ref, sin_ref = refs[:2]
        out_refs = refs[2:]
    else:
        out_refs = refs
    x = x_ref[...]
    ms = jnp.mean(x * x, axis=-1, keepdims=True)
    xn = ((x * lax.rsqrt(ms + RMS_EPS)) * nw_ref[...]).astype(BF16)
    off = 0
    for wd, o, rs in zip(widths, out_refs, rope_scales):
        for c0 in range(0, wd, 512):
            cw = min(512, wd - c0)
            v = jnp.dot(xn, w_ref[:, off + c0:off + c0 + cw], preferred_element_type=F32)
            if rs is not None:
                v = _rope(v, cos_ref[...], sin_ref[...]) * rs
            o[:, c0:c0 + cw] = v.astype(o.dtype)
        off += wd


def _norm_proj(x, nw, w, widths, dtypes, rope_scales=None, seq=None):
    t, d = x.shape
    n = w.shape[1]
    assert sum(widths) == n and t % ROW_TILE == 0
    rope_scales = tuple(rope_scales) if rope_scales is not None else (None,) * len(widths)
    in_specs = [pl.BlockSpec((ROW_TILE, d), lambda i: (i, 0)),
                pl.BlockSpec((1, d), lambda i: (0, 0)),
                pl.BlockSpec((d, n), lambda i: (0, 0))]
    args = [x, nw.reshape(1, d), w]
    if any(s is not None for s in rope_scales):
        ta, la, lb = seq
        cos, sin = _rope_tables(max(la, lb))
        tab = pl.BlockSpec((ROW_TILE, LANES), lambda i: (_seq_pos(i * ROW_TILE, ta, la, lb)[0] // ROW_TILE, 0))
        in_specs += [tab, tab]
        args += [cos, sin]
    return pl.pallas_call(
        functools.partial(_norm_proj_kernel, widths=tuple(widths), rope_scales=rope_scales),
        grid=(t // ROW_TILE,),
        in_specs=in_specs,
        out_specs=[pl.BlockSpec((ROW_TILE, wd), lambda i: (i, 0)) for wd in widths],
        out_shape=[jax.ShapeDtypeStruct((t, wd), dt) for wd, dt in zip(widths, dtypes)],
        compiler_params=_cparams(1),
        name="norm_proj",
    )(*args)


def _halo_specs(r, width, t):
    hb = r // BF16_ROWS
    last = t // BF16_ROWS - 1
    return [pl.BlockSpec((BF16_ROWS, width), lambda i: (jnp.maximum(i * hb - 1, 0), 0)),
            pl.BlockSpec((r, width), lambda i: (i, 0)),
            pl.BlockSpec((BF16_ROWS, width), lambda i: (jnp.minimum((i + 1) * hb, last), 0))]


def _fill_halo_buf(buf, prev_ref, cur_ref, next_ref, r, has_prev, has_next):
    buf[0:BF16_ROWS, :] = jnp.where(has_prev, prev_ref[...].astype(F32), 0.0)
    buf[BF16_ROWS:BF16_ROWS + r, :] = cur_ref[...].astype(F32)
    buf[BF16_ROWS + r:2 * BF16_ROWS + r, :] = jnp.where(has_next, next_ref[...].astype(F32), 0.0)


def _ssd_conv_kernel(prev_ref, cur_ref, next_ref, w_ref, b_ref, xs_ref, bm_ref, cm_ref, buf,
                     *, r, ta, la, lb):
    i = pl.program_id(0)
    pos, seqlen = _seq_pos(i * r, ta, la, lb)
    _fill_halo_buf(buf, prev_ref, cur_ref, next_ref, r, pos > 0, pos + r < seqlen)
    half = SSD_TAPS // 2
    for c0 in range(0, SSD_CONV_DIM, 512):
        for r0 in range(0, r, 128):
            acc = jnp.broadcast_to(b_ref[:, c0:c0 + 512], (128, 512))
            for k in range(SSD_TAPS):
                start = BF16_ROWS + r0 + k - half
                acc = acc + w_ref[k:k + 1, c0:c0 + 512] * buf[start:start + 128, c0:c0 + 512]
            y = _silu(acc).astype(BF16)
            if c0 < SSD_INNER:
                xs_ref[r0:r0 + 128, c0:c0 + 512] = y
            elif c0 < SSD_INNER + SSD_BC:
                bm_ref[r0:r0 + 128, :] = y
            else:
                cm_ref[r0:r0 + 128, :] = y


def _ssd_conv(xbc, conv_w, conv_b, seq):
    t = xbc.shape[0]
    r = ROW_TILE
    return pl.pallas_call(
        functools.partial(_ssd_conv_kernel, r=r, ta=seq[0], la=seq[1], lb=seq[2]),
        grid=(t // r,),
        in_specs=_halo_specs(r, SSD_CONV_DIM, t) + [
            pl.BlockSpec((SSD_TAPS, SSD_CONV_DIM), lambda i: (0, 0)),
            pl.BlockSpec((1, SSD_CONV_DIM), lambda i: (0, 0))],
        out_specs=[pl.BlockSpec((r, SSD_INNER), lambda i: (i, 0)),
                   pl.BlockSpec((r, SSD_BC), lambda i: (i, 0)),
                   pl.BlockSpec((r, SSD_BC), lambda i: (i, 0))],
        out_shape=[jax.ShapeDtypeStruct((t, SSD_INNER), BF16),
                   jax.ShapeDtypeStruct((t, SSD_BC), BF16),
                   jax.ShapeDtypeStruct((t, SSD_BC), BF16)],
        scratch_shapes=[pltpu.VMEM((r + 2 * BF16_ROWS, SSD_CONV_DIM), F32)],
        compiler_params=_cparams(1),
        name="ssd_conv",
    )(xbc, xbc, xbc, conv_w, conv_b.reshape(1, SSD_CONV_DIM))


def _chunk_iotas():
    ri = lax.broadcasted_iota(I32, (CHUNK, CHUNK), 0)
    ci = lax.broadcasted_iota(I32, (CHUNK, CHUNK), 1)
    return ri, ci


def _col(v, lane):
    return jnp.broadcast_to(v[:, lane:lane + 1], (CHUNK, LANES))


def _pair(left, a, b):
    return jnp.where(left, a, b)


def _ssd_bwd_kernel(x_ref, b_ref, c_ref, dt_ref, bias_ref, alog_ref, out_ref, h_ref, xw_ref,
                    *, n_chunks, ta, la, lb):
    c = n_chunks - 1 - pl.program_id(0)
    pos, seqlen = _seq_pos(c * CHUNK, ta, la, lb)

    @pl.when(pos + CHUNK == seqlen)
    def _():
        h_ref[...] = jnp.zeros_like(h_ref)

    ri, ci = _chunk_iotas()
    left = ci < SSD_HEAD_DIM
    dt_all = _softplus(dt_ref[...] + bias_ref[...])
    da = dt_all * (-jnp.exp(alog_ref[...]))
    sfx = _tri_dot(ci >= ri, da)
    wc = dt_all * jnp.exp(sfx[0:1, :] - sfx)
    for g in range(SSD_GROUPS):
        gs = slice(g * 384, (g + 1) * 384)
        cg = c_ref[:, g * SSD_STATE:(g + 1) * SSD_STATE]
        bg = b_ref[:, g * SSD_STATE:(g + 1) * SSD_STATE]
        hg = h_ref[:, gs]
        yoff = jnp.dot(cg, hg.astype(BF16), preferred_element_type=F32)
        cds = []
        for jj in range(3):
            j = g * 3 + jj
            ls = slice(j * LANES, (j + 1) * LANES)
            h0 = SSD_HEADS + 2 * j
            es = jnp.exp(_pair(left, _col(sfx, h0), _col(sfx, h0 + 1)))
            wt = _pair(left, _col(wc, h0), _col(wc, h0 + 1))
            xw_ref[:, ls] = (x_ref[:, ls].astype(F32) * wt).astype(BF16)
            out_ref[:, ls] = (yoff[:, jj * LANES:(jj + 1) * LANES] * es).astype(out_ref.dtype)
            cds.append(es[0:1, :])
        cd = jnp.concatenate(cds, axis=1)
        h_ref[:, gs] = hg * cd + _tn_dot(bg, xw_ref[:, gs])


def _ssd_main_kernel(x_ref, b_ref, c_ref, z_ref, dt_ref, ybo_ref, bias_ref, alog_ref, dskip_ref,
                     gw_ref, out_ref, h_ref, xw_ref, y_ref, rows_ref, *, ta, la, lb):
    pos, _ = _seq_pos(pl.program_id(0) * CHUNK, ta, la, lb)

    @pl.when(pos == 0)
    def _():
        h_ref[...] = jnp.zeros_like(h_ref)

    ri, ci = _chunk_iotas()
    left = ci < SSD_HEAD_DIM
    lower = ci <= ri
    low_s = ci < ri
    up_s = ci > ri
    dt_all = _softplus(dt_ref[...] + bias_ref[...])
    da = dt_all * (-jnp.exp(alog_ref[...]))
    acs = _tri_dot(lower, da)
    sfx = _tri_dot(ci >= ri, da)
    wc = dt_all * jnp.exp(acs[CHUNK - 1:CHUNK, :] - acs)
    rows_ref[0] = acs.T
    rows_ref[1] = sfx.T
    rows_ref[2] = dt_all.T

    def row(k, h):
        return jnp.broadcast_to(rows_ref[k, h:h + 1, :], (CHUNK, CHUNK))

    for g in range(SSD_GROUPS):
        gs = slice(g * 384, (g + 1) * 384)
        cg = c_ref[:, g * SSD_STATE:(g + 1) * SSD_STATE]
        bg = b_ref[:, g * SSD_STATE:(g + 1) * SSD_STATE]
        cb = lax.dot_general(cg, bg, (((1,), (1,)), ((), ())), preferred_element_type=F32)
        hg = h_ref[:, gs]
        yoff = jnp.dot(cg, hg.astype(BF16), preferred_element_type=F32)
        cds = []
        for jj in range(3):
            j = g * 3 + jj
            ls = slice(j * LANES, (j + 1) * LANES)
            xt = x_ref[:, ls]
            xf = xt.astype(F32)
            yd = jnp.zeros((CHUNK, LANES), F32)
            cols = []
            for s in range(2):
                h = 2 * j + s
                hb = SSD_HEADS + h
                colf = _col(acs, h)
                cols.append(colf)
                arg = jnp.where(lower, colf - row(0, h), _col(sfx, hb) - row(1, hb))
                dtf = row(2, h)
                dtb = row(2, hb)
                dsel = jnp.where(low_s, dtf, jnp.where(up_s, dtb, dtf + dtb))
                m = (cb * jnp.exp(arg) * dsel).astype(BF16)
                xm = jnp.where(left if s == 0 else jnp.logical_not(left), xf, 0.0).astype(BF16)
                yd = yd + jnp.dot(m, xm, preferred_element_type=F32)
            es = jnp.exp(_pair(left, cols[0], cols[1]))
            wt = _pair(left, _col(wc, 2 * j), _col(wc, 2 * j + 1))
            xw_ref[:, ls] = (xf * wt).astype(BF16)
            y_ref[:, ls] = (yd + yoff[:, jj * LANES:(jj + 1) * LANES] * es
                            + ybo_ref[:, ls].astype(F32) + xf * dskip_ref[:, ls])
            cds.append(es[CHUNK - 1:CHUNK, :])
        cd = jnp.concatenate(cds, axis=1)
        h_ref[:, gs] = hg * cd + _tn_dot(bg, xw_ref[:, gs])

    y = y_ref[...] * _silu(z_ref[...].astype(F32))
    ms = jnp.mean(y * y, axis=-1, keepdims=True)
    out_ref[...] = ((y * lax.rsqrt(ms + RMS_EPS)) * gw_ref[...]).astype(out_ref.dtype)


def _ssd_scan(xs, bm, cm, z, dt, dt_bias, a_log, d_skip, gnorm_w, seq):
    t = xs.shape[0]
    n = t // CHUNK
    ta, la, lb = seq
    pad = LANES - 2 * SSD_HEADS
    bias = jnp.pad(dt_bias.reshape(1, 2 * SSD_HEADS), ((0, 0), (0, pad)))
    alog = jnp.pad(a_log.reshape(1, 2 * SSD_HEADS), ((0, 0), (0, pad)))
    dexp = jnp.repeat(d_skip, SSD_HEAD_DIM).reshape(1, SSD_INNER)
    small = lambda w: pl.BlockSpec((1, w), lambda i: (0, 0))

    rev = lambda i: (n - 1 - i, 0)
    ybo = pl.pallas_call(
        functools.partial(_ssd_bwd_kernel, n_chunks=n, ta=ta, la=la, lb=lb),
        grid=(n,),
        in_specs=[pl.BlockSpec((CHUNK, SSD_INNER), rev),
                  pl.BlockSpec((CHUNK, SSD_BC), rev),
                  pl.BlockSpec((CHUNK, SSD_BC), rev),
                  pl.BlockSpec((CHUNK, LANES), rev),
                  small(LANES), small(LANES)],
        out_specs=pl.BlockSpec((CHUNK, SSD_INNER), rev),
        out_shape=jax.ShapeDtypeStruct((t, SSD_INNER), BF16),
        scratch_shapes=[pltpu.VMEM((SSD_STATE, SSD_INNER), F32),
                        pltpu.VMEM((CHUNK, SSD_INNER), BF16)],
        compiler_params=_cparams(1, "arbitrary"),
        name="ssd_bwd",
    )(xs, bm, cm, dt, bias, alog)

    fwd = lambda i: (i, 0)
    return pl.pallas_call(
        functools.partial(_ssd_main_kernel, ta=ta, la=la, lb=lb),
        grid=(n,),
        in_specs=[pl.BlockSpec((CHUNK, SSD_INNER), fwd),
                  pl.BlockSpec((CHUNK, SSD_BC), fwd),
                  pl.BlockSpec((CHUNK, SSD_BC), fwd),
                  pl.BlockSpec((CHUNK, SSD_INNER), fwd),
                  pl.BlockSpec((CHUNK, LANES), fwd),
                  pl.BlockSpec((CHUNK, SSD_INNER), fwd),
                  small(LANES), small(LANES), small(SSD_INNER), small(SSD_INNER)],
        out_specs=pl.BlockSpec((CHUNK, SSD_INNER), fwd),
        out_shape=jax.ShapeDtypeStruct((t, SSD_INNER), BF16),
        scratch_shapes=[pltpu.VMEM((SSD_STATE, SSD_INNER), F32),
                        pltpu.VMEM((CHUNK, SSD_INNER), BF16),
                        pltpu.VMEM((CHUNK, SSD_INNER), F32),
                        pltpu.VMEM((3, CHUNK, CHUNK), F32)],
        compiler_params=_cparams(1, "arbitrary"),
        name="ssd_main",
    )(xs, bm, cm, z, dt, ybo, bias, alog, dexp, gnorm_w.reshape(1, SSD_INNER))


def _pool_kernel(prev_ref, cur_ref, next_ref, w_ref, sc_ref, out_ref, buf, *, r, ta, la, lb):
    i = pl.program_id(0)
    pos, seqlen = _seq_pos(i * r, ta, la, lb)
    _fill_halo_buf(buf, prev_ref, cur_ref, next_ref, r, pos > 0, pos + r < seqlen)
    for gi, w in enumerate(POOL_WINDOWS):
        ls = slice(gi * LANES, (gi + 1) * LANES)
        for r0 in range(0, r, 128):
            base = BF16_ROWS + r0
            acc = buf[base - w // 2:base - w // 2 + 128, ls]
            for k in range(1, w):
                s = base - w // 2 + k
                acc = acc + buf[s:s + 128, ls]
            tpos = pos + r0 + lax.broadcasted_iota(I32, (128, LANES), 0)
            lo = jnp.maximum(tpos - w // 2, 0)
            hi = jnp.minimum(tpos - w // 2 + w, seqlen)
            mean = acc / (hi - lo).astype(F32)
            diff = (mean - buf[base:base + 128, ls]).astype(BF16)
            o = jnp.dot(diff, w_ref[gi], preferred_element_type=F32) * sc_ref[:, ls]
            out_ref[r0:r0 + 128, ls] = o.astype(out_ref.dtype)


def _pool_mixer(xp, pool_w, pool_scale, seq):
    t = xp.shape[0]
    r = ROW_TILE
    return pl.pallas_call(
        functools.partial(_pool_kernel, r=r, ta=seq[0], la=seq[1], lb=seq[2]),
        grid=(t // r,),
        in_specs=_halo_specs(r, POOL_WIDTH, t) + [
            pl.BlockSpec((len(POOL_WINDOWS), LANES, LANES), lambda i: (0, 0, 0)),
            pl.BlockSpec((1, POOL_WIDTH), lambda i: (0, 0))],
        out_specs=pl.BlockSpec((r, POOL_WIDTH), lambda i: (i, 0)),
        out_shape=jax.ShapeDtypeStruct((t, POOL_WIDTH), BF16),
        scratch_shapes=[pltpu.VMEM((r + 2 * BF16_ROWS, POOL_WIDTH), F32)],
        compiler_params=_cparams(1),
        name="pool_mixer",
    )(xp, xp, xp, pool_w.astype(BF16), pool_scale.reshape(1, POOL_WIDTH))


def _short_conv_kernel(gcp, gcc, gcn, xp, xc, xn, gb_ref, w_ref, out_ref, buf, *, r, ta, la, lb):
    i = pl.program_id(0)
    pos, seqlen = _seq_pos(i * r, ta, la, lb)
    has_prev = pos > 0
    has_next = pos + r < seqlen
    prod = lambda a, b: a[...].astype(F32) * b[...].astype(F32)
    buf[0:BF16_ROWS, :] = jnp.where(has_prev, prod(gcp, xp), 0.0)
    buf[BF16_ROWS:BF16_ROWS + r, :] = prod(gcc, xc)
    buf[BF16_ROWS + r:2 * BF16_ROWS + r, :] = jnp.where(has_next, prod(gcn, xn), 0.0)
    for c0 in range(0, SC_WIDTH, 512):
        cs = slice(c0, c0 + 512)
        for r0 in range(0, r, 128):
            base = BF16_ROWS + r0
            acc = w_ref[0:1, cs] * buf[base - 1:base + 127, cs]
            acc = acc + w_ref[1:2, cs] * buf[base:base + 128, cs]
            acc = acc + w_ref[2:3, cs] * buf[base + 1:base + 129, cs]
            out_ref[r0:r0 + 128, cs] = (gb_ref[r0:r0 + 128, cs].astype(F32) * acc).astype(out_ref.dtype)


def _short_conv(gb, gc, xc, conv_w, seq):
    t = gb.shape[0]
    r = ROW_TILE
    halo = _halo_specs(r, SC_WIDTH, t)
    return pl.pallas_call(
        functools.partial(_short_conv_kernel, r=r, ta=seq[0], la=seq[1], lb=seq[2]),
        grid=(t // r,),
        in_specs=halo + halo + [pl.BlockSpec((r, SC_WIDTH), lambda i: (i, 0)),
                                pl.BlockSpec((3, SC_WIDTH), lambda i: (0, 0))],
        out_specs=pl.BlockSpec((r, SC_WIDTH), lambda i: (i, 0)),
        out_shape=jax.ShapeDtypeStruct((t, SC_WIDTH), BF16),
        scratch_shapes=[pltpu.VMEM((r + 2 * BF16_ROWS, SC_WIDTH), F32)],
        compiler_params=_cparams(1),
        name="short_conv",
    )(gc, gc, gc, xc, xc, xc, gb, conv_w)


LOG2E = 1.4426950408889634


def _attn_kernel(q_ref, kp_ref, kc_ref, kn_ref, vp_ref, vc_ref, vn_ref, sink_ref, out_ref, *, ta, la, lb):
    pos, seqlen = _seq_pos(pl.program_id(0) * CHUNK, ta, la, lb)
    has_prev = pos > 0
    has_next = pos + CHUNK < seqlen
    rows = 2 * CHUNK
    qi = lax.broadcasted_iota(I32, (rows, 3 * CHUNK), 0) % CHUNK
    kk = lax.broadcasted_iota(I32, (rows, 3 * CHUNK), 1)
    mask = (kk >= qi) & (kk <= qi + 2 * WINDOW)
    mask = mask & ((kk >= CHUNK) | has_prev) & ((kk < 2 * CHUNK) | has_next)
    bias = jnp.where(mask, 0.0, -jnp.inf)
    lo_q = lax.broadcasted_iota(I32, (rows, LANES), 1) < HEAD_DIM
    lo_v = lax.broadcasted_iota(I32, (3 * CHUNK, LANES), 1) < HEAD_DIM
    top = lax.broadcasted_iota(I32, (rows, 1), 0) < CHUNK
    nt = (((1,), (1,)), ((), ()))
    zq = jnp.zeros((rows, LANES), BF16)
    zv = jnp.zeros((3 * CHUNK, LANES), BF16)
    for g in range(N_KV_HEADS):
        ls = slice(g * LANES, (g + 1) * LANES)
        kd = jnp.concatenate([kp_ref[:, ls], kc_ref[:, ls], kn_ref[:, ls]], axis=0)
        vd = jnp.concatenate([vp_ref[:, ls], vc_ref[:, ls], vn_ref[:, ls]], axis=0)
        q2 = jnp.concatenate([q_ref[:, (2 * g) * LANES:(2 * g + 1) * LANES],
                              q_ref[:, (2 * g + 1) * LANES:(2 * g + 2) * LANES]], axis=0)
        acc = jnp.zeros((rows, LANES), F32)
        for half in range(2):
            sel_q = lo_q if half == 0 else jnp.logical_not(lo_q)
            sel_v = lo_v if half == 0 else jnp.logical_not(lo_v)
            ha = 4 * g + half
            sink = jnp.where(top, sink_ref[:, ha:ha + 1], sink_ref[:, ha + 2:ha + 3]) * LOG2E
            s = lax.dot_general(jnp.where(sel_q, q2, zq), kd, nt, preferred_element_type=F32) + bias
            m = jnp.maximum(jnp.max(s, axis=-1, keepdims=True), sink)
            p = jnp.exp2(s - m)
            den = jnp.sum(p, axis=-1, keepdims=True) + jnp.exp2(sink - m)
            o = jnp.dot(p.astype(BF16), jnp.where(sel_v, vd, zv), preferred_element_type=F32)
            acc = acc + o * (1.0 / den)
        out_ref[:, (2 * g) * LANES:(2 * g + 1) * LANES] = acc[0:CHUNK].astype(out_ref.dtype)
        out_ref[:, (2 * g + 1) * LANES:(2 * g + 2) * LANES] = acc[CHUNK:rows].astype(out_ref.dtype)


def _rope_tables(lmax):
    inv = 1.0 / (ROPE_THETA ** (jnp.arange(0, HEAD_DIM, 2, dtype=F32) / HEAD_DIM))
    ang = jnp.arange(lmax, dtype=F32)[:, None] * inv[None, :]
    cos, sin = jnp.cos(ang), jnp.sin(ang)
    cos = jnp.concatenate([cos, cos, cos, cos], axis=1)
    sin = jnp.concatenate([-sin, sin, -sin, sin], axis=1)
    return cos, sin


def _attention(q, kd, vd, sinks, seq):
    t = q.shape[0]
    n = t // CHUNK
    ta, la, lb = seq
    nlast = n - 1
    cur = lambda i: (i, 0)
    prv = lambda i: (jnp.maximum(i - 1, 0), 0)
    nxt = lambda i: (jnp.minimum(i + 1, nlast), 0)
    kv = lambda f: pl.BlockSpec((CHUNK, 2 * KV_DIM), f)
    return pl.pallas_call(
        functools.partial(_attn_kernel, ta=ta, la=la, lb=lb),
        grid=(n,),
        in_specs=[pl.BlockSpec((CHUNK, Q_DIM), cur), kv(prv), kv(cur), kv(nxt), kv(prv), kv(cur), kv(nxt),
                  pl.BlockSpec((1, N_Q_HEADS), lambda i: (0, 0))],
        out_specs=pl.BlockSpec((CHUNK, Q_DIM), cur),
        out_shape=jax.ShapeDtypeStruct((t, Q_DIM), BF16),
        compiler_params=_cparams(1),
        name="band_attention",
    )(q, kd, kd, kd, vd, vd, vd, sinks.reshape(1, N_Q_HEADS))


def _out_router_kernel(a_ref, b_ref, h_ref, wa_ref, wb_ref, nw_ref, wr_ref, br_ref,
                       h_out, eidx_out, wts_out, hist_out):
    h1 = (h_ref[...] + jnp.dot(a_ref[...], wa_ref[...], preferred_element_type=F32)
          + jnp.dot(b_ref[...], wb_ref[...], preferred_element_type=F32))
    h_out[...] = h1
    ms = jnp.mean(h1 * h1, axis=-1, keepdims=True)
    tn = (h1 * lax.rsqrt(ms + RMS_EPS)) * nw_ref[...]
    t_hi, t_mid, _ = _split3(tn)
    w_hi, w_mid, _ = _split3(wr_ref[...])
    nt = (((1,), (1,)), ((), ()))
    logits = (lax.dot_general(w_hi, t_hi, nt, preferred_element_type=F32)
              + lax.dot_general(w_hi, t_mid, nt, preferred_element_type=F32)
              + lax.dot_general(w_mid, t_hi, nt, preferred_element_type=F32)) + br_ref[...]
    tm = logits.shape[1]
    row = lax.broadcasted_iota(I32, (PER_GROUP, tm), 0)
    gl = jnp.where(row < N_GROUPS, logits[0:PER_GROUP], -jnp.inf)
    gmax = jnp.max(gl, axis=0, keepdims=True)
    gsel = jnp.min(jnp.where(gl == gmax, row, PER_GROUP), axis=0, keepdims=True)
    p_group = 1.0 / jnp.sum(jnp.exp(gl - gmax), axis=0, keepdims=True)
    el = logits[PER_GROUP:2 * PER_GROUP]
    for gi in range(1, N_GROUPS):
        el = jnp.where(gsel == gi, logits[(gi + 1) * PER_GROUP:(gi + 2) * PER_GROUP], el)
    m1 = jnp.max(el, axis=0, keepdims=True)
    i1 = jnp.min(jnp.where(el == m1, row, PER_GROUP), axis=0, keepdims=True)
    el2 = jnp.where(row == i1, -jnp.inf, el)
    m2 = jnp.max(el2, axis=0, keepdims=True)
    i2 = jnp.min(jnp.where(el2 == m2, row, PER_GROUP), axis=0, keepdims=True)
    ratio = jnp.exp(m2 - m1)
    w1 = p_group / (1.0 + ratio)
    w2 = p_group * ratio / (1.0 + ratio)
    ex1 = gsel * PER_GROUP + i1
    ex2 = gsel * PER_GROUP + i2
    eidx_out[...] = jnp.where(row == 0, ex1, jnp.where(row == 1, ex2, 0))
    erow = lax.broadcasted_iota(I32, (N_EXPERTS, tm), 0)
    hits = jnp.where((erow == ex1) | (erow == ex2), 1.0, 0.0)
    hist_out[0] = jnp.broadcast_to(jnp.sum(hits, axis=1, keepdims=True), (N_EXPERTS, LANES))
    wrow = lax.broadcasted_iota(I32, (LANES, tm), 0)
    wmat = jnp.where(wrow == 0, w1, jnp.where(wrow == 1, w2, 0.0))
    wts_out[...] = wmat.T


def _out_router(a, b, h, w_out, nw, w_group, b_group, w_expert, b_expert):
    t, d = h.shape
    ka, kb = a.shape[1], b.shape[1]
    wa = w_out[:ka].astype(BF16)
    wb = w_out[ka:].astype(BF16)
    wr = jnp.zeros((LANES, d), F32).at[0:N_GROUPS].set(w_group.T).at[PER_GROUP:PER_GROUP + N_EXPERTS].set(w_expert.T)
    br = jnp.zeros((LANES, 1), F32).at[0:N_GROUPS, 0].set(b_group).at[PER_GROUP:PER_GROUP + N_EXPERTS, 0].set(b_expert)
    r = ROW_TILE
    row_spec = lambda w: pl.BlockSpec((r, w), lambda i: (i, 0))
    const = lambda s: pl.BlockSpec(s, lambda i: (0, 0))
    return pl.pallas_call(
        _out_router_kernel,
        grid=(t // r,),
        in_specs=[row_spec(ka), row_spec(kb), row_spec(d), const((ka, d)), const((kb, d)), const((1, d)),
                  const((LANES, d)), const((LANES, 1))],
        out_specs=[row_spec(d), pl.BlockSpec((PER_GROUP, r), lambda i: (0, i)), row_spec(LANES),
                   pl.BlockSpec((1, N_EXPERTS, LANES), lambda i: (i, 0, 0))],
        out_shape=[jax.ShapeDtypeStruct((t, d), F32), jax.ShapeDtypeStruct((PER_GROUP, t), I32),
                   jax.ShapeDtypeStruct((t, LANES), F32),
                   jax.ShapeDtypeStruct((t // r, N_EXPERTS, LANES), F32)],
        compiler_params=_cparams(1),
        name="out_proj_router",
    )(a, b, h, wa, wb, nw.reshape(1, d), wr, br)


def _slot_kernel(eidx_ref, base_ref, dest_out):
    tm = eidx_ref.shape[1]
    erow = lax.broadcasted_iota(I32, (N_EXPERTS, tm), 0)
    ti = lax.broadcasted_iota(I32, (tm, tm), 0)
    tj = lax.broadcasted_iota(I32, (tm, tm), 1)
    before = jnp.where(ti < tj, 1.0, 0.0).astype(BF16)
    base = base_ref[0][:, 0:1]
    slots = []
    for k in range(2):
        oh = jnp.where(eidx_ref[k:k + 1, :] == erow, 1.0, 0.0)
        prefix = jnp.dot(oh.astype(BF16), before, preferred_element_type=F32)
        slots.append(jnp.sum(oh * (prefix + base), axis=0, keepdims=True))
        base = base + jnp.sum(oh, axis=1, keepdims=True)
    row = lax.broadcasted_iota(I32, (PER_GROUP, tm), 0)
    dest_out[0] = jnp.where(row == 0, slots[0], jnp.where(row == 1, slots[1], 0.0)).astype(I32)


def _slots(eidx, base):
    t = eidx.shape[1]
    r = ROW_TILE
    return pl.pallas_call(
        _slot_kernel,
        grid=(t // r,),
        in_specs=[pl.BlockSpec((PER_GROUP, r), lambda i: (0, i)),
                  pl.BlockSpec((1, N_EXPERTS, LANES), lambda i: (i, 0, 0))],
        out_specs=pl.BlockSpec((1, PER_GROUP, r), lambda i: (i, 0, 0)),
        out_shape=jax.ShapeDtypeStruct((t // r, PER_GROUP, r), I32),
        compiler_params=_cparams(1),
        name="moe_slots",
    )(eidx, base)


def _dispatch_kernel(dest_ref, x_ref, init_ref, out_ref, sem):
    del init_ref
    tm = x_ref.shape[0]

    def copy(r, k):
        return pltpu.make_async_copy(x_ref.at[pl.ds(r, 1)], out_ref.at[pl.ds(dest_ref[0, k, r], 1)], sem)

    def start(r, c):
        copy(r, 0).start()
        copy(r, 1).start()
        return c

    def wait(r, c):
        copy(r, 0).wait()
        copy(r, 1).wait()
        return c

    lax.fori_loop(0, tm, start, 0, unroll=DMA_UNROLL)
    lax.fori_loop(0, tm, wait, 0, unroll=DMA_UNROLL)


def _dispatch(dest3, h, n_slots):
    t, d = h.shape
    r = ROW_TILE
    init = jnp.zeros((n_slots, d), F32)
    return pl.pallas_call(
        _dispatch_kernel,
        grid=(t // r,),
        in_specs=[pl.BlockSpec((1, PER_GROUP, r), lambda i: (i, 0, 0), memory_space=pltpu.SMEM),
                  pl.BlockSpec((r, d), lambda i: (i, 0)),
                  pl.BlockSpec(memory_space=pl.ANY)],
        out_specs=pl.BlockSpec(memory_space=pl.ANY),
        out_shape=jax.ShapeDtypeStruct((n_slots, d), F32),
        scratch_shapes=[pltpu.SemaphoreType.DMA],
        input_output_aliases={2: 0},
        compiler_params=_cparams(1, "arbitrary"),
        name="moe_dispatch",
    )(dest3, h, init)


def _expert_kernel(be_ref, nb_ref, x_ref, nw_ref, w1_ref, w3_ref, w2_ref, out_ref, w1b, w3b, w2b):
    i = pl.program_id(0)
    prev = be_ref[jnp.maximum(i - 1, 0)]

    @pl.when((i == 0) | (be_ref[i] != prev))
    def _():
        w1b[...] = w1_ref[0].astype(BF16)
        w3b[...] = w3_ref[0].astype(BF16)
        w2b[...] = w2_ref[0].astype(BF16)

    @pl.when(i < nb_ref[0])
    def _():
        x = x_ref[...]
        ms = jnp.mean(x * x, axis=-1, keepdims=True)
        xn = ((x * lax.rsqrt(ms + RMS_EPS)) * nw_ref[...]).astype(BF16)
        a = jnp.dot(xn, w1b[...], preferred_element_type=F32)
        b = jnp.dot(xn, w3b[...], preferred_element_type=F32)
        mid = (_silu(a) * b).astype(BF16)
        out_ref[...] = jnp.dot(mid, w2b[...], preferred_element_type=F32)

    @pl.when(i >= nb_ref[0])
    def _():
        out_ref[...] = jnp.zeros_like(out_ref)


def _experts(xin, block_e, n_used, nw, w1, w3, w2):
    n_slots, d = xin.shape
    nb = n_slots // MOE_BLOCK
    f = w1.shape[2]
    grid_spec = pltpu.PrefetchScalarGridSpec(
        num_scalar_prefetch=2,
        grid=(nb,),
        in_specs=[pl.BlockSpec((MOE_BLOCK, d), lambda i, be, nu: (i, 0)),
                  pl.BlockSpec((1, d), lambda i, be, nu: (0, 0)),
                  pl.BlockSpec((1, d, f), lambda i, be, nu: (be[i], 0, 0)),
                  pl.BlockSpec((1, d, f), lambda i, be, nu: (be[i], 0, 0)),
                  pl.BlockSpec((1, f, d), lambda i, be, nu: (be[i], 0, 0))],
        out_specs=pl.BlockSpec((MOE_BLOCK, d), lambda i, be, nu: (i, 0)),
        scratch_shapes=[pltpu.VMEM((d, f), BF16), pltpu.VMEM((d, f), BF16), pltpu.VMEM((f, d), BF16)],
    )
    return pl.pallas_call(
        _expert_kernel,
        grid_spec=grid_spec,
        out_shape=jax.ShapeDtypeStruct((n_slots, d), F32),
        compiler_params=_cparams(1, "arbitrary"),
        name="moe_experts",
    )(block_e, n_used, xin, nw.reshape(1, d), w1, w3, w2)


def _combine_kernel(dest_ref, h_ref, wts_ref, y_ref, nw_ref, out_ref, gbuf, sem, *, final_norm):
    tm = h_ref.shape[0]

    def copy(r, k):
        return pltpu.make_async_copy(y_ref.at[pl.ds(dest_ref[0, k, r], 1)], gbuf.at[k, pl.ds(r, 1)], sem)

    def start(r, c):
        copy(r, 0).start()
        copy(r, 1).start()
        return c

    def wait(r, c):
        copy(r, 0).wait()
        copy(r, 1).wait()
        return c

    lax.fori_loop(0, tm, start, 0, unroll=DMA_UNROLL)
    lax.fori_loop(0, tm, wait, 0, unroll=DMA_UNROLL)
    wts = wts_ref[...]
    out = h_ref[...] + (wts[:, 0:1] * gbuf[0] + wts[:, 1:2] * gbuf[1])
    if final_norm:
        ms = jnp.mean(out * out, axis=-1, keepdims=True)
        out = (out * lax.rsqrt(ms + RMS_EPS)) * nw_ref[...]
    out_ref[...] = out


def _combine(dest3, h, wts, yout, nw, final_norm):
    t, d = h.shape
    r = ROW_TILE
    return pl.pallas_call(
        functools.partial(_combine_kernel, final_norm=final_norm),
        grid=(t // r,),
        in_specs=[pl.BlockSpec((1, PER_GROUP, r), lambda i: (i, 0, 0), memory_space=pltpu.SMEM),
                  pl.BlockSpec((r, d), lambda i: (i, 0)),
                  pl.BlockSpec((r, LANES), lambda i: (i, 0)),
                  pl.BlockSpec(memory_space=pl.ANY),
                  pl.BlockSpec((1, d), lambda i: (0, 0))],
        out_specs=pl.BlockSpec((r, d), lambda i: (i, 0)),
        out_shape=jax.ShapeDtypeStruct((t, d), F32),
        scratch_shapes=[pltpu.VMEM((2, r, d), F32), pltpu.SemaphoreType.DMA],
        compiler_params=_cparams(1, "arbitrary"),
        name="moe_combine",
    )(dest3, h, wts, yout, nw.reshape(1, d))


def _moe(h1, eidx, wts, hist, norm_w, w1, w3, w2, final_nw):
    t, d = h1.shape
    tile_counts = hist[:, :, 0].astype(I32)
    counts = jnp.sum(tile_counts, axis=0)
    padded = (counts + MOE_BLOCK - 1) // MOE_BLOCK * MOE_BLOCK
    pad_end = jnp.cumsum(padded)
    pad_start = pad_end - padded
    n_blocks = -(-(2 * t) // MOE_BLOCK) + N_EXPERTS
    n_used = (pad_end[-1] // MOE_BLOCK).astype(I32).reshape(1)
    blk = jnp.minimum(jnp.arange(n_blocks, dtype=I32), n_used[0] - 1) * MOE_BLOCK
    block_e = jnp.minimum(jnp.sum((pad_end[None, :] <= blk[:, None]).astype(I32), axis=1), N_EXPERTS - 1)
    tile_base = pad_start[None, :] + jnp.cumsum(tile_counts, axis=0) - tile_counts
    base = jnp.broadcast_to(tile_base.astype(F32)[:, :, None], tile_base.shape + (LANES,))
    dest3 = _slots(eidx, base)
    xin = _dispatch(dest3, h1, n_blocks * MOE_BLOCK)
    yout = _experts(xin, block_e, n_used, norm_w, w1, w3, w2)
    nw = final_nw if final_nw is not None else norm_w
    return _combine(dest3, h1, wts, yout, nw, final_nw is not None)


def _even_mixers(x, seq, p):
    wi = p["even_w_in"][0]
    z_end = SSD_INNER
    xbc_end = z_end + SSD_CONV_DIM
    dt_end = xbc_end + 2 * SSD_HEADS
    w0 = jnp.concatenate([wi[:, :xbc_end], wi[:, dt_end:], wi[:, xbc_end:dt_end],
                          jnp.zeros((D_MODEL, LANES - 2 * SSD_HEADS), F32)], axis=1).astype(BF16)
    z, xbc, xp, dt = _norm_proj(x, p["norm_mix"][0], w0, (SSD_INNER, SSD_CONV_DIM, POOL_WIDTH, LANES),
                                (BF16, BF16, BF16, F32))
    xs, bm, cm = _ssd_conv(xbc, p["ssd_conv_w"][0], p["ssd_conv_b"][0], seq)
    y_ssd = _ssd_scan(xs, bm, cm, z, dt, p["ssd_dt_bias"][0], p["ssd_A_log"][0], p["ssd_D"][0],
                      p["ssd_norm_w"][0], seq)
    y_pool = _pool_mixer(xp, p["pool_w"][0], p["pool_scale"][0], seq)
    return y_ssd, y_pool


def _odd_mixers(h, seq, p):
    wo = p["odd_w_in"][0]
    qkv0 = 3 * SC_WIDTH + Q_DIM

    def dup_heads(w):
        return jnp.repeat(w.reshape(D_MODEL, N_KV_HEADS, 1, HEAD_DIM), 2, axis=2).reshape(D_MODEL, 2 * KV_DIM)

    w1 = jnp.concatenate([wo[:, :qkv0], dup_heads(wo[:, qkv0:qkv0 + KV_DIM]),
                          dup_heads(wo[:, qkv0 + KV_DIM:])], axis=1).astype(BF16)
    gb, gc, xc, q, kd, vd = _norm_proj(
        h, p["norm_mix"][1], w1, (SC_WIDTH, SC_WIDTH, SC_WIDTH, Q_DIM, 2 * KV_DIM, 2 * KV_DIM), (BF16,) * 6,
        rope_scales=(None, None, None, HEAD_DIM ** -0.5 * LOG2E, 1.0, None), seq=seq)
    y_conv = _short_conv(gb, gc, xc, p["sc_conv_w"][0], seq)
    y_attn = _attention(q, kd, vd, p["attn_sinks"][0], seq)
    return y_conv, y_attn


def _trunk(x, seq, p):
    y_ssd, y_pool = _even_mixers(x, seq, p)
    h, eidx, wts, hist = _out_router(y_ssd, y_pool, x, p["even_w_out"][0], p["norm_ffn"][0],
                                     p["moe_w_group"][0], p["moe_b_group"][0], p["moe_w_expert"][0],
                                     p["moe_b_expert"][0])
    h = _moe(h, eidx, wts, hist, p["norm_ffn"][0], p["moe_w1"][0], p["moe_w3"][0], p["moe_w2"][0], None)
    y_conv, y_attn = _odd_mixers(h, seq, p)
    h, eidx, wts, hist = _out_router(y_conv, y_attn, h, p["odd_w_out"][0], p["norm_ffn"][1],
                                     p["moe_w_group"][1], p["moe_b_group"][1], p["moe_w_expert"][1],
                                     p["moe_b_expert"][1])
    return _moe(h, eidx, wts, hist, p["norm_ffn"][1], p["moe_w1"][1], p["moe_w3"][1], p["moe_w2"][1],
                p["norm_final"])


def kernel(x_prompt, x_sample, norm_mix, norm_ffn, norm_final, even_w_in, ssd_conv_w, ssd_conv_b, ssd_A_log,
           ssd_dt_bias, ssd_D, ssd_norm_w, pool_w, pool_scale, even_w_out, odd_w_in, sc_conv_w, attn_sinks,
           odd_w_out, moe_w_group, moe_b_group, moe_w_expert, moe_b_expert, moe_w1, moe_w3, moe_w2):
    p = dict(norm_mix=norm_mix, norm_ffn=norm_ffn, norm_final=norm_final, even_w_in=even_w_in,
             ssd_conv_w=ssd_conv_w, ssd_conv_b=ssd_conv_b, ssd_A_log=ssd_A_log, ssd_dt_bias=ssd_dt_bias,
             ssd_D=ssd_D, ssd_norm_w=ssd_norm_w, pool_w=pool_w, pool_scale=pool_scale, even_w_out=even_w_out,
             odd_w_in=odd_w_in, sc_conv_w=sc_conv_w, attn_sinks=attn_sinks, odd_w_out=odd_w_out,
             moe_w_group=moe_w_group, moe_b_group=moe_b_group, moe_w_expert=moe_w_expert,
             moe_b_expert=moe_b_expert, moe_w1=moe_w1, moe_w3=moe_w3, moe_w2=moe_w2)
    bp, lp, d = x_prompt.shape
    bs, ls, _ = x_sample.shape
    ta = bp * lp
    x = jnp.concatenate([x_prompt.reshape(ta, d), x_sample.reshape(bs * ls, d)], axis=0)
    y = _trunk(x, (ta, lp, ls), p)
    return y[:ta].reshape(bp, lp, d), y[ta:].reshape(bs, ls, d)
```

```python
import functools

import jax
import jax.numpy as jnp
from jax import lax
from jax.experimental import pallas as pl
from jax.experimental.pallas import tpu as pltpu

F32 = jnp.float32
BF16 = jnp.bfloat16
I32 = jnp.int32

RMS_EPS = 1e-6
D_MODEL = 1024
LANES = 128
BF16_ROWS = 16
VMEM_LIMIT = 56 * 1024 * 1024

SSD_HEADS = 24
SSD_HEAD_DIM = 64
SSD_INNER = SSD_HEADS * SSD_HEAD_DIM
SSD_GROUPS = 4
SSD_STATE = 128
SSD_BC = SSD_GROUPS * SSD_STATE
SSD_CONV_DIM = SSD_INNER + 2 * SSD_BC
SSD_TAPS = 5
CHUNK = 128
POOL_WINDOWS = (2, 4, 8, 16)
POOL_WIDTH = 512
SC_WIDTH = 1024
N_Q_HEADS = 16
N_KV_HEADS = 4
HEAD_DIM = 64
Q_DIM = N_Q_HEADS * HEAD_DIM
KV_DIM = N_KV_HEADS * HEAD_DIM
WINDOW = 128
ROPE_THETA = 10000.0
N_GROUPS = 4
PER_GROUP = 8
N_EXPERTS = N_GROUPS * PER_GROUP
D_EXPERT = 512
MOE_BLOCK = 512
ROW_TILE = 512
DMA_UNROLL = 8


def _cparams(n_grid, semantics="parallel"):
    return pltpu.CompilerParams(dimension_semantics=(semantics,) * n_grid, vmem_limit_bytes=VMEM_LIMIT)


def _seq_pos(tok0, ta, la, lb):
    in_a = tok0 < ta
    seqlen = jnp.where(in_a, la, lb)
    pos = jnp.where(in_a, tok0 % la, (tok0 - ta) % lb)
    return pos, seqlen


def _row_parts(x):
    return tuple(x) if isinstance(x, (tuple, list)) else (x,)


def _row_in_specs(parts, r, w):
    if len(parts) == 1:
        return [pl.BlockSpec((r, w), lambda i: (i, 0))]
    na = parts[0].shape[0] // r
    return [pl.BlockSpec((r, w), lambda i: (jnp.minimum(i, na - 1), 0)),
            pl.BlockSpec((r, w), lambda i: (jnp.maximum(i - na, 0), 0))]


def _load_rows(refs, na):
    if len(refs) == 1:
        return refs[0][...]
    return jnp.where(pl.program_id(0) < na, refs[0][...], refs[1][...])


def _sigmoid(x):
    return 1.0 / (1.0 + jnp.exp(-x))


def _silu(x):
    return x * _sigmoid(x)


def _softplus(x):
    return jnp.maximum(x, 0.0) + jnp.log1p(jnp.exp(-jnp.abs(x)))


def _split3(x):
    hi = x.astype(BF16)
    r1 = x - hi.astype(F32)
    mid = r1.astype(BF16)
    lo = (r1 - mid.astype(F32)).astype(BF16)
    return hi, mid, lo


def _tri_dot(mask, v):
    m = jnp.where(mask, 1.0, 0.0).astype(BF16)
    hi, mid, lo = _split3(v)
    acc = jnp.dot(m, hi, preferred_element_type=F32)
    acc += jnp.dot(m, mid, preferred_element_type=F32)
    acc += jnp.dot(m, lo, preferred_element_type=F32)
    return acc


def _tn_dot(a, b):
    at = a.astype(F32).T.astype(BF16)
    return jnp.dot(at, b, preferred_element_type=F32)


def _rope(t, cos, sin_signed):
    n = t.shape[1]
    lane = lax.broadcasted_iota(I32, t.shape, 1)
    first = (lane % HEAD_DIM) < HEAD_DIM // 2
    rot = jnp.where(first, pltpu.roll(t, n - HEAD_DIM // 2, 1), pltpu.roll(t, HEAD_DIM // 2, 1))
    reps = n // LANES
    return t * jnp.tile(cos, (1, reps)) + rot * jnp.tile(sin_signed, (1, reps))


def _norm_proj_kernel(*refs, n_x, na, widths, rope_scales):
    x_refs, (nw_ref, w_ref), refs = refs[:n_x], refs[n_x:n_x + 2], refs[n_x + 2:]
    use_rope = any(s is not None for s in rope_scales)
    if use_rope:
        cos_ref, sin_ref = refs[:2]
        out_refs = refs[2:]
    else:
        out_refs = refs
    x = _load_rows(x_refs, na)
    ms = jnp.mean(x * x, axis=-1, keepdims=True)
    xn = ((x * lax.rsqrt(ms + RMS_EPS)) * nw_ref[...]).astype(BF16)
    off = 0
    for wd, o, rs in zip(widths, out_refs, rope_scales):
        for c0 in range(0, wd, 512):
            cw = min(512, wd - c0)
            v = jnp.dot(xn, w_ref[:, off + c0:off + c0 + cw], preferred_element_type=F32)
            if rs is not None:
                v = _rope(v, cos_ref[...], sin_ref[...]) * rs
            o[:, c0:c0 + cw] = v.astype(o.dtype)
        off += wd


def _norm_proj(x, nw, w, widths, dtypes, rope_scales=None, seq=None):
    parts = _row_parts(x)
    d = parts[0].shape[1]
    t = sum(p.shape[0] for p in parts)
    n = w.shape[1]
    assert sum(widths) == n and all(p.shape[0] % ROW_TILE == 0 for p in parts)
    rope_scales = tuple(rope_scales) if rope_scales is not None else (None,) * len(widths)
    in_specs = _row_in_specs(parts, ROW_TILE, d) + [pl.BlockSpec((1, d), lambda i: (0, 0)),
                                                    pl.BlockSpec((d, n), lambda i: (0, 0))]
    args = list(parts) + [nw.reshape(1, d), w]
    if any(s is not None for s in rope_scales):
        ta, la, lb = seq
        cos, sin = _rope_tables(max(la, lb))
        tab = pl.BlockSpec((ROW_TILE, LANES), lambda i: (_seq_pos(i * ROW_TILE, ta, la, lb)[0] // ROW_TILE, 0))
        in_specs += [tab, tab]
        args += [cos, sin]
    return pl.pallas_call(
        functools.partial(_norm_proj_kernel, n_x=len(parts), na=parts[0].shape[0] // ROW_TILE,
                          widths=tuple(widths), rope_scales=rope_scales),
        grid=(t // ROW_TILE,),
        in_specs=in_specs,
        out_specs=[pl.BlockSpec((ROW_TILE, wd), lambda i: (i, 0)) for wd in widths],
        out_shape=[jax.ShapeDtypeStruct((t, wd), dt) for wd, dt in zip(widths, dtypes)],
        compiler_params=_cparams(1),
        name="norm_proj",
    )(*args)


def _halo_specs(r, width, t):
    hb = r // BF16_ROWS
    last = t // BF16_ROWS - 1
    return [pl.BlockSpec((BF16_ROWS, width), lambda i: (jnp.maximum(i * hb - 1, 0), 0)),
            pl.BlockSpec((r, width), lambda i: (i, 0)),
            pl.BlockSpec((BF16_ROWS, width), lambda i: (jnp.minimum((i + 1) * hb, last), 0))]


def _fill_halo_buf(buf, prev_ref, cur_ref, next_ref, r, has_prev, has_next):
    buf[0:BF16_ROWS, :] = jnp.where(has_prev, prev_ref[...].astype(F32), 0.0)
    buf[BF16_ROWS:BF16_ROWS + r, :] = cur_ref[...].astype(F32)
    buf[BF16_ROWS + r:2 * BF16_ROWS + r, :] = jnp.where(has_next, next_ref[...].astype(F32), 0.0)


def _ssd_conv_kernel(prev_ref, cur_ref, next_ref, w_ref, b_ref, xs_ref, bm_ref, cm_ref, buf,
                     *, r, ta, la, lb):
    i = pl.program_id(0)
    pos, seqlen = _seq_pos(i * r, ta, la, lb)
    _fill_halo_buf(buf, prev_ref, cur_ref, next_ref, r, pos > 0, pos + r < seqlen)
    half = SSD_TAPS // 2
    for c0 in range(0, SSD_CONV_DIM, 512):
        for r0 in range(0, r, 128):
            acc = jnp.broadcast_to(b_ref[:, c0:c0 + 512], (128, 512))
            for k in range(SSD_TAPS):
                start = BF16_ROWS + r0 + k - half
                acc = acc + w_ref[k:k + 1, c0:c0 + 512] * buf[start:start + 128, c0:c0 + 512]
            y = _silu(acc).astype(BF16)
            if c0 < SSD_INNER:
                xs_ref[r0:r0 + 128, c0:c0 + 512] = y
            elif c0 < SSD_INNER + SSD_BC:
                bm_ref[r0:r0 + 128, :] = y
            else:
                cm_ref[r0:r0 + 128, :] = y


def _ssd_conv(xbc, conv_w, conv_b, seq):
    t = xbc.shape[0]
    r = ROW_TILE
    return pl.pallas_call(
        functools.partial(_ssd_conv_kernel, r=r, ta=seq[0], la=seq[1], lb=seq[2]),
        grid=(t // r,),
        in_specs=_halo_specs(r, SSD_CONV_DIM, t) + [
            pl.BlockSpec((SSD_TAPS, SSD_CONV_DIM), lambda i: (0, 0)),
            pl.BlockSpec((1, SSD_CONV_DIM), lambda i: (0, 0))],
        out_specs=[pl.BlockSpec((r, SSD_INNER), lambda i: (i, 0)),
                   pl.BlockSpec((r, SSD_BC), lambda i: (i, 0)),
                   pl.BlockSpec((r, SSD_BC), lambda i: (i, 0))],
        out_shape=[jax.ShapeDtypeStruct((t, SSD_INNER), BF16),
                   jax.ShapeDtypeStruct((t, SSD_BC), BF16),
                   jax.ShapeDtypeStruct((t, SSD_BC), BF16)],
        scratch_shapes=[pltpu.VMEM((r + 2 * BF16_ROWS, SSD_CONV_DIM), F32)],
        compiler_params=_cparams(1),
        name="ssd_conv",
    )(xbc, xbc, xbc, conv_w, conv_b.reshape(1, SSD_CONV_DIM))


def _chunk_iotas():
    ri = lax.broadcasted_iota(I32, (CHUNK, CHUNK), 0)
    ci = lax.broadcasted_iota(I32, (CHUNK, CHUNK), 1)
    return ri, ci


def _col(v, lane):
    return jnp.broadcast_to(v[:, lane:lane + 1], (CHUNK, LANES))


def _pair(left, a, b):
    return jnp.where(left, a, b)


def _ssd_bwd_kernel(x_ref, b_ref, c_ref, dt_ref, bias_ref, alog_ref, out_ref, h_ref, xw_ref,
                    *, n_chunks, ta, la, lb):
    c = n_chunks - 1 - pl.program_id(0)
    pos, seqlen = _seq_pos(c * CHUNK, ta, la, lb)

    @pl.when(pos + CHUNK == seqlen)
    def _():
        h_ref[...] = jnp.zeros_like(h_ref)

    ri, ci = _chunk_iotas()
    left = ci < SSD_HEAD_DIM
    dt_all = _softplus(dt_ref[...] + bias_ref[...])
    da = dt_all * (-jnp.exp(alog_ref[...]))
    sfx = _tri_dot(ci >= ri, da)
    wc = dt_all * jnp.exp(sfx[0:1, :] - sfx)
    for g in range(SSD_GROUPS):
        gs = slice(g * 384, (g + 1) * 384)
        cg = c_ref[:, g * SSD_STATE:(g + 1) * SSD_STATE]
        bg = b_ref[:, g * SSD_STATE:(g + 1) * SSD_STATE]
        hg = h_ref[:, gs]
        yoff = jnp.dot(cg, hg.astype(BF16), preferred_element_type=F32)
        cds = []
        for jj in range(3):
            j = g * 3 + jj
            ls = slice(j * LANES, (j + 1) * LANES)
            h0 = SSD_HEADS + 2 * j
            es = jnp.exp(_pair(left, _col(sfx, h0), _col(sfx, h0 + 1)))
            wt = _pair(left, _col(wc, h0), _col(wc, h0 + 1))
            xw_ref[:, ls] = (x_ref[:, ls].astype(F32) * wt).astype(BF16)
            out_ref[:, ls] = (yoff[:, jj * LANES:(jj + 1) * LANES] * es).astype(out_ref.dtype)
            cds.append(es[0:1, :])
        cd = jnp.concatenate(cds, axis=1)
        h_ref[:, gs] = hg * cd + _tn_dot(bg, xw_ref[:, gs])


def _ssd_main_kernel(x_ref, b_ref, c_ref, z_ref, dt_ref, ybo_ref, bias_ref, alog_ref, dskip_ref,
                     gw_ref, out_ref, h_ref, xw_ref, y_ref, rows_ref, *, ta, la, lb):
    pos, _ = _seq_pos(pl.program_id(0) * CHUNK, ta, la, lb)

    @pl.when(pos == 0)
    def _():
        h_ref[...] = jnp.zeros_like(h_ref)

    ri, ci = _chunk_iotas()
    left = ci < SSD_HEAD_DIM
    lower = ci <= ri
    low_s = ci < ri
    up_s = ci > ri
    dt_all = _softplus(dt_ref[...] + bias_ref[...])
    da = dt_all * (-jnp.exp(alog_ref[...]))
    acs = _tri_dot(lower, da)
    sfx = _tri_dot(ci >= ri, da)
    wc = dt_all * jnp.exp(acs[CHUNK - 1:CHUNK, :] - acs)
    rows_ref[0] = acs.T
    rows_ref[1] = sfx.T
    rows_ref[2] = dt_all.T

    def row(k, h):
        return jnp.broadcast_to(rows_ref[k, h:h + 1, :], (CHUNK, CHUNK))

    for g in range(SSD_GROUPS):
        gs = slice(g * 384, (g + 1) * 384)
        cg = c_ref[:, g * SSD_STATE:(g + 1) * SSD_STATE]
        bg = b_ref[:, g * SSD_STATE:(g + 1) * SSD_STATE]
        cb = lax.dot_general(cg, bg, (((1,), (1,)), ((), ())), preferred_element_type=F32)
        hg = h_ref[:, gs]
        yoff = jnp.dot(cg, hg.astype(BF16), preferred_element_type=F32)
        cds = []
        for jj in range(3):
            j = g * 3 + jj
            ls = slice(j * LANES, (j + 1) * LANES)
            xt = x_ref[:, ls]
            xf = xt.astype(F32)
            yd = jnp.zeros((CHUNK, LANES), F32)
            cols = []
            for s in range(2):
                h = 2 * j + s
                hb = SSD_HEADS + h
                colf = _col(acs, h)
                cols.append(colf)
                arg = jnp.where(lower, colf - row(0, h), _col(sfx, hb) - row(1, hb))
                dtf = row(2, h)
                dtb = row(2, hb)
                dsel = jnp.where(low_s, dtf, jnp.where(up_s, dtb, dtf + dtb))
                m = (cb * jnp.exp(arg) * dsel).astype(BF16)
                xm = jnp.where(left if s == 0 else jnp.logical_not(left), xf, 0.0).astype(BF16)
                yd = yd + jnp.dot(m, xm, preferred_element_type=F32)
            es = jnp.exp(_pair(left, cols[0], cols[1]))
            wt = _pair(left, _col(wc, 2 * j), _col(wc, 2 * j + 1))
            xw_ref[:, ls] = (xf * wt).astype(BF16)
            y_ref[:, ls] = (yd + yoff[:, jj * LANES:(jj + 1) * LANES] * es
                            + ybo_ref[:, ls].astype(F32) + xf * dskip_ref[:, ls])
            cds.append(es[CHUNK - 1:CHUNK, :])
        cd = jnp.concatenate(cds, axis=1)
        h_ref[:, gs] = hg * cd + _tn_dot(bg, xw_ref[:, gs])

    y = y_ref[...] * _silu(z_ref[...].astype(F32))
    ms = jnp.mean(y * y, axis=-1, keepdims=True)
    out_ref[...] = ((y * lax.rsqrt(ms + RMS_EPS)) * gw_ref[...]).astype(out_ref.dtype)


def _ssd_scan(xs, bm, cm, z, dt, dt_bias, a_log, d_skip, gnorm_w, seq):
    t = xs.shape[0]
    n = t // CHUNK
    ta, la, lb = seq
    pad = LANES - 2 * SSD_HEADS
    bias = jnp.pad(dt_bias.reshape(1, 2 * SSD_HEADS), ((0, 0), (0, pad)))
    alog = jnp.pad(a_log.reshape(1, 2 * SSD_HEADS), ((0, 0), (0, pad)))
    dexp = jnp.repeat(d_skip, SSD_HEAD_DIM).reshape(1, SSD_INNER)
    small = lambda w: pl.BlockSpec((1, w), lambda i: (0, 0))

    rev = lambda i: (n - 1 - i, 0)
    ybo = pl.pallas_call(
        functools.partial(_ssd_bwd_kernel, n_chunks=n, ta=ta, la=la, lb=lb),
        grid=(n,),
        in_specs=[pl.BlockSpec((CHUNK, SSD_INNER), rev),
                  pl.BlockSpec((CHUNK, SSD_BC), rev),
                  pl.BlockSpec((CHUNK, SSD_BC), rev),
                  pl.BlockSpec((CHUNK, LANES), rev),
                  small(LANES), small(LANES)],
        out_specs=pl.BlockSpec((CHUNK, SSD_INNER), rev),
        out_shape=jax.ShapeDtypeStruct((t, SSD_INNER), BF16),
        scratch_shapes=[pltpu.VMEM((SSD_STATE, SSD_INNER), F32),
                        pltpu.VMEM((CHUNK, SSD_INNER), BF16)],
        compiler_params=_cparams(1, "arbitrary"),
        name="ssd_bwd",
    )(xs, bm, cm, dt, bias, alog)

    fwd = lambda i: (i, 0)
    return pl.pallas_call(
        functools.partial(_ssd_main_kernel, ta=ta, la=la, lb=lb),
        grid=(n,),
        in_specs=[pl.BlockSpec((CHUNK, SSD_INNER), fwd),
                  pl.BlockSpec((CHUNK, SSD_BC), fwd),
                  pl.BlockSpec((CHUNK, SSD_BC), fwd),
                  pl.BlockSpec((CHUNK, SSD_INNER), fwd),
                  pl.BlockSpec((CHUNK, LANES), fwd),
                  pl.BlockSpec((CHUNK, SSD_INNER), fwd),
                  small(LANES), small(LANES), small(SSD_INNER), small(SSD_INNER)],
        out_specs=pl.BlockSpec((CHUNK, SSD_INNER), fwd),
        out_shape=jax.ShapeDtypeStruct((t, SSD_INNER), BF16),
        scratch_shapes=[pltpu.VMEM((SSD_STATE, SSD_INNER), F32),
                        pltpu.VMEM((CHUNK, SSD_INNER), BF16),
                        pltpu.VMEM((CHUNK, SSD_INNER), F32),
                        pltpu.VMEM((3, CHUNK, CHUNK), F32)],
        compiler_params=_cparams(1, "arbitrary"),
        name="ssd_main",
    )(xs, bm, cm, z, dt, ybo, bias, alog, dexp, gnorm_w.reshape(1, SSD_INNER))


def _pool_kernel(prev_ref, cur_ref, next_ref, w_ref, sc_ref, out_ref, buf, *, r, ta, la, lb):
    i = pl.program_id(0)
    pos, seqlen = _seq_pos(i * r, ta, la, lb)
    _fill_halo_buf(buf, prev_ref, cur_ref, next_ref, r, pos > 0, pos + r < seqlen)
    for gi, w in enumerate(POOL_WINDOWS):
        ls = slice(gi * LANES, (gi + 1) * LANES)
        for r0 in range(0, r, 128):
            base = BF16_ROWS + r0
            acc = buf[base - w // 2:base - w // 2 + 128, ls]
            for k in range(1, w):
                s = base - w // 2 + k
                acc = acc + buf[s:s + 128, ls]
            tpos = pos + r0 + lax.broadcasted_iota(I32, (128, LANES), 0)
            lo = jnp.maximum(tpos - w // 2, 0)
            hi = jnp.minimum(tpos - w // 2 + w, seqlen)
            mean = acc / (hi - lo).astype(F32)
            diff = (mean - buf[base:base + 128, ls]).astype(BF16)
            o = jnp.dot(diff, w_ref[gi], preferred_element_type=F32) * sc_ref[:, ls]
            out_ref[r0:r0 + 128, ls] = o.astype(out_ref.dtype)


def _pool_mixer(xp, pool_w, pool_scale, seq):
    t = xp.shape[0]
    r = ROW_TILE
    return pl.pallas_call(
        functools.partial(_pool_kernel, r=r, ta=seq[0], la=seq[1], lb=seq[2]),
        grid=(t // r,),
        in_specs=_halo_specs(r, POOL_WIDTH, t) + [
            pl.BlockSpec((len(POOL_WINDOWS), LANES, LANES), lambda i: (0, 0, 0)),
            pl.BlockSpec((1, POOL_WIDTH), lambda i: (0, 0))],
        out_specs=pl.BlockSpec((r, POOL_WIDTH), lambda i: (i, 0)),
        out_shape=jax.ShapeDtypeStruct((t, POOL_WIDTH), BF16),
        scratch_shapes=[pltpu.VMEM((r + 2 * BF16_ROWS, POOL_WIDTH), F32)],
        compiler_params=_cparams(1),
        name="pool_mixer",
    )(xp, xp, xp, pool_w.astype(BF16), pool_scale.reshape(1, POOL_WIDTH))


def _short_conv_kernel(gcp, gcc, gcn, xp, xc, xn, gb_ref, w_ref, out_ref, buf, *, r, ta, la, lb):
    i = pl.program_id(0)
    pos, seqlen = _seq_pos(i * r, ta, la, lb)
    has_prev = pos > 0
    has_next = pos + r < seqlen
    prod = lambda a, b: a[...].astype(F32) * b[...].astype(F32)
    buf[0:BF16_ROWS, :] = jnp.where(has_prev, prod(gcp, xp), 0.0)
    buf[BF16_ROWS:BF16_ROWS + r, :] = prod(gcc, xc)
    buf[BF16_ROWS + r:2 * BF16_ROWS + r, :] = jnp.where(has_next, prod(gcn, xn), 0.0)
    for c0 in range(0, SC_WIDTH, 512):
        cs = slice(c0, c0 + 512)
        for r0 in range(0, r, 128):
            base = BF16_ROWS + r0
            acc = w_ref[0:1, cs] * buf[base - 1:base + 127, cs]
            acc = acc + w_ref[1:2, cs] * buf[base:base + 128, cs]
            acc = acc + w_ref[2:3, cs] * buf[base + 1:base + 129, cs]
            out_ref[r0:r0 + 128, cs] = (gb_ref[r0:r0 + 128, cs].astype(F32) * acc).astype(out_ref.dtype)


def _short_conv(gb, gc, xc, conv_w, seq):
    t = gb.shape[0]
    r = ROW_TILE
    halo = _halo_specs(r, SC_WIDTH, t)
    return pl.pallas_call(
        functools.partial(_short_conv_kernel, r=r, ta=seq[0], la=seq[1], lb=seq[2]),
        grid=(t // r,),
        in_specs=halo + halo + [pl.BlockSpec((r, SC_WIDTH), lambda i: (i, 0)),
                                pl.BlockSpec((3, SC_WIDTH), lambda i: (0, 0))],
        out_specs=pl.BlockSpec((r, SC_WIDTH), lambda i: (i, 0)),
        out_shape=jax.ShapeDtypeStruct((t, SC_WIDTH), BF16),
        scratch_shapes=[pltpu.VMEM((r + 2 * BF16_ROWS, SC_WIDTH), F32)],
        compiler_params=_cparams(1),
        name="short_conv",
    )(gc, gc, gc, xc, xc, xc, gb, conv_w)


LOG2E = 1.4426950408889634


def _attn_kernel(q_ref, kp_ref, kc_ref, kn_ref, vp_ref, vc_ref, vn_ref, sink_ref, out_ref, *, ta, la, lb):
    pos, seqlen = _seq_pos(pl.program_id(0) * CHUNK, ta, la, lb)
    has_prev = pos > 0
    has_next = pos + CHUNK < seqlen
    rows = 2 * CHUNK
    qi = lax.broadcasted_iota(I32, (rows, 3 * CHUNK), 0) % CHUNK
    kk = lax.broadcasted_iota(I32, (rows, 3 * CHUNK), 1)
    mask = (kk >= qi) & (kk <= qi + 2 * WINDOW)
    mask = mask & ((kk >= CHUNK) | has_prev) & ((kk < 2 * CHUNK) | has_next)
    bias = jnp.where(mask, 0.0, -jnp.inf)
    lo_q = lax.broadcasted_iota(I32, (rows, LANES), 1) < HEAD_DIM
    lo_v = lax.broadcasted_iota(I32, (3 * CHUNK, LANES), 1) < HEAD_DIM
    top = lax.broadcasted_iota(I32, (rows, 1), 0) < CHUNK
    nt = (((1,), (1,)), ((), ()))
    zq = jnp.zeros((rows, LANES), BF16)
    zv = jnp.zeros((3 * CHUNK, LANES), BF16)
    for g in range(N_KV_HEADS):
        ls = slice(g * LANES, (g + 1) * LANES)
        kd = jnp.concatenate([kp_ref[:, ls], kc_ref[:, ls], kn_ref[:, ls]], axis=0)
        vd = jnp.concatenate([vp_ref[:, ls], vc_ref[:, ls], vn_ref[:, ls]], axis=0)
        q2 = jnp.concatenate([q_ref[:, (2 * g) * LANES:(2 * g + 1) * LANES],
                              q_ref[:, (2 * g + 1) * LANES:(2 * g + 2) * LANES]], axis=0)
        acc = jnp.zeros((rows, LANES), F32)
        for half in range(2):
            sel_q = lo_q if half == 0 else jnp.logical_not(lo_q)
            sel_v = lo_v if half == 0 else jnp.logical_not(lo_v)
            ha = 4 * g + half
            sink = jnp.where(top, sink_ref[:, ha:ha + 1], sink_ref[:, ha + 2:ha + 3]) * LOG2E
            s = lax.dot_general(jnp.where(sel_q, q2, zq), kd, nt, preferred_element_type=F32) + bias
            m = jnp.maximum(jnp.max(s, axis=-1, keepdims=True), sink)
            p = jnp.exp2(s - m)
            den = jnp.sum(p, axis=-1, keepdims=True) + jnp.exp2(sink - m)
            o = jnp.dot(p.astype(BF16), jnp.where(sel_v, vd, zv), preferred_element_type=F32)
            acc = acc + o * (1.0 / den)
        out_ref[:, (2 * g) * LANES:(2 * g + 1) * LANES] = acc[0:CHUNK].astype(out_ref.dtype)
        out_ref[:, (2 * g + 1) * LANES:(2 * g + 2) * LANES] = acc[CHUNK:rows].astype(out_ref.dtype)


def _rope_tables(lmax):
    inv = 1.0 / (ROPE_THETA ** (jnp.arange(0, HEAD_DIM, 2, dtype=F32) / HEAD_DIM))
    ang = jnp.arange(lmax, dtype=F32)[:, None] * inv[None, :]
    cos, sin = jnp.cos(ang), jnp.sin(ang)
    cos = jnp.concatenate([cos, cos, cos, cos], axis=1)
    sin = jnp.concatenate([-sin, sin, -sin, sin], axis=1)
    return cos, sin


def _attention(q, kd, vd, sinks, seq):
    t = q.shape[0]
    n = t // CHUNK
    ta, la, lb = seq
    nlast = n - 1
    cur = lambda i: (i, 0)
    prv = lambda i: (jnp.maximum(i - 1, 0), 0)
    nxt = lambda i: (jnp.minimum(i + 1, nlast), 0)
    kv = lambda f: pl.BlockSpec((CHUNK, 2 * KV_DIM), f)
    return pl.pallas_call(
        functools.partial(_attn_kernel, ta=ta, la=la, lb=lb),
        grid=(n,),
        in_specs=[pl.BlockSpec((CHUNK, Q_DIM), cur), kv(prv), kv(cur), kv(nxt), kv(prv), kv(cur), kv(nxt),
                  pl.BlockSpec((1, N_Q_HEADS), lambda i: (0, 0))],
        out_specs=pl.BlockSpec((CHUNK, Q_DIM), cur),
        out_shape=jax.ShapeDtypeStruct((t, Q_DIM), BF16),
        compiler_params=_cparams(1),
        name="band_attention",
    )(q, kd, kd, kd, vd, vd, vd, sinks.reshape(1, N_Q_HEADS))


def _out_router_kernel(a_ref, b_ref, *refs, n_h, na):
    h_refs, refs = refs[:n_h], refs[n_h:]
    wa_ref, wb_ref, nw_ref, wr_ref, br_ref, h_out, hrow_out, eidx_out, wts_out, hist_out = refs
    h1 = (_load_rows(h_refs, na) + jnp.dot(a_ref[...], wa_ref[...], preferred_element_type=F32)
          + jnp.dot(b_ref[...], wb_ref[...], preferred_element_type=F32))
    h_out[...] = h1
    hrow_out[...] = h1.reshape(hrow_out.shape)
    ms = jnp.mean(h1 * h1, axis=-1, keepdims=True)
    tn = (h1 * lax.rsqrt(ms + RMS_EPS)) * nw_ref[...]
    t_hi, t_mid, _ = _split3(tn)
    w_hi, w_mid, _ = _split3(wr_ref[...])
    nt = (((1,), (1,)), ((), ()))
    logits = (lax.dot_general(w_hi, t_hi, nt, preferred_element_type=F32)
              + lax.dot_general(w_hi, t_mid, nt, preferred_element_type=F32)
              + lax.dot_general(w_mid, t_hi, nt, preferred_element_type=F32)) + br_ref[...]
    tm = logits.shape[1]
    row = lax.broadcasted_iota(I32, (PER_GROUP, tm), 0)
    gl = jnp.where(row < N_GROUPS, logits[0:PER_GROUP], -jnp.inf)
    gmax = jnp.max(gl, axis=0, keepdims=True)
    gsel = jnp.min(jnp.where(gl == gmax, row, PER_GROUP), axis=0, keepdims=True)
    p_group = 1.0 / jnp.sum(jnp.exp(gl - gmax), axis=0, keepdims=True)
    el = logits[PER_GROUP:2 * PER_GROUP]
    for gi in range(1, N_GROUPS):
        el = jnp.where(gsel == gi, logits[(gi + 1) * PER_GROUP:(gi + 2) * PER_GROUP], el)
    m1 = jnp.max(el, axis=0, keepdims=True)
    i1 = jnp.min(jnp.where(el == m1, row, PER_GROUP), axis=0, keepdims=True)
    el2 = jnp.where(row == i1, -jnp.inf, el)
    m2 = jnp.max(el2, axis=0, keepdims=True)
    i2 = jnp.min(jnp.where(el2 == m2, row, PER_GROUP), axis=0, keepdims=True)
    ratio = jnp.exp(m2 - m1)
    w1 = p_group / (1.0 + ratio)
    w2 = p_group * ratio / (1.0 + ratio)
    ex1 = gsel * PER_GROUP + i1
    ex2 = gsel * PER_GROUP + i2
    eidx_out[...] = jnp.where(row == 0, ex1, jnp.where(row == 1, ex2, 0))
    erow = lax.broadcasted_iota(I32, (N_EXPERTS, tm), 0)
    hits = jnp.where((erow == ex1) | (erow == ex2), 1.0, 0.0)
    hist_out[0] = jnp.broadcast_to(jnp.sum(hits, axis=1, keepdims=True), (N_EXPERTS, LANES))
    wrow = lax.broadcasted_iota(I32, (LANES, tm), 0)
    wmat = jnp.where(wrow == 0, w1, jnp.where(wrow == 1, w2, 0.0))
    wts_out[...] = wmat.T


ROW_SUB = D_MODEL // LANES


def _out_router(a, b, h, w_out, nw, w_group, b_group, w_expert, b_expert):
    h_parts = _row_parts(h)
    d = h_parts[0].shape[1]
    t = a.shape[0]
    ka, kb = a.shape[1], b.shape[1]
    wa = w_out[:ka].astype(BF16)
    wb = w_out[ka:].astype(BF16)
    wr = jnp.zeros((LANES, d), F32).at[0:N_GROUPS].set(w_group.T).at[PER_GROUP:PER_GROUP + N_EXPERTS].set(w_expert.T)
    br = jnp.zeros((LANES, 1), F32).at[0:N_GROUPS, 0].set(b_group).at[PER_GROUP:PER_GROUP + N_EXPERTS, 0].set(b_expert)
    r = ROW_TILE
    row_spec = lambda w: pl.BlockSpec((r, w), lambda i: (i, 0))
    const = lambda s: pl.BlockSpec(s, lambda i: (0, 0))
    return pl.pallas_call(
        functools.partial(_out_router_kernel, n_h=len(h_parts), na=h_parts[0].shape[0] // r),
        grid=(t // r,),
        in_specs=[row_spec(ka), row_spec(kb)] + _row_in_specs(h_parts, r, d) + [
            const((ka, d)), const((kb, d)), const((1, d)), const((LANES, d)), const((LANES, 1))],
        out_specs=[row_spec(d), pl.BlockSpec((r, ROW_SUB, LANES), lambda i: (i, 0, 0)),
                   pl.BlockSpec((PER_GROUP, r), lambda i: (0, i)), row_spec(LANES),
                   pl.BlockSpec((1, N_EXPERTS, LANES), lambda i: (i, 0, 0))],
        out_shape=[jax.ShapeDtypeStruct((t, d), F32), jax.ShapeDtypeStruct((t, ROW_SUB, LANES), F32),
                   jax.ShapeDtypeStruct((PER_GROUP, t), I32), jax.ShapeDtypeStruct((t, LANES), F32),
                   jax.ShapeDtypeStruct((t // r, N_EXPERTS, LANES), F32)],
        compiler_params=_cparams(1),
        name="out_proj_router",
    )(a, b, *h_parts, wa, wb, nw.reshape(1, d), wr, br)


def _slot_kernel(eidx_ref, base_ref, dest_out):
    tm = eidx_ref.shape[1]
    erow = lax.broadcasted_iota(I32, (N_EXPERTS, tm), 0)
    ti = lax.broadcasted_iota(I32, (tm, tm), 0)
    tj = lax.broadcasted_iota(I32, (tm, tm), 1)
    before = jnp.where(ti < tj, 1.0, 0.0).astype(BF16)
    base = base_ref[0][:, 0:1]
    slots = []
    for k in range(2):
        oh = jnp.where(eidx_ref[k:k + 1, :] == erow, 1.0, 0.0)
        prefix = jnp.dot(oh.astype(BF16), before, preferred_element_type=F32)
        slots.append(jnp.sum(oh * (prefix + base), axis=0, keepdims=True))
        base = base + jnp.sum(oh, axis=1, keepdims=True)
    row = lax.broadcasted_iota(I32, (PER_GROUP, tm), 0)
    dest_out[0] = jnp.where(row == 0, slots[0], jnp.where(row == 1, slots[1], 0.0)).astype(I32)


def _slots(eidx, base):
    t = eidx.shape[1]
    r = ROW_TILE
    return pl.pallas_call(
        _slot_kernel,
        grid=(t // r,),
        in_specs=[pl.BlockSpec((PER_GROUP, r), lambda i: (0, i)),
                  pl.BlockSpec((1, N_EXPERTS, LANES), lambda i: (i, 0, 0))],
        out_specs=pl.BlockSpec((1, PER_GROUP, r), lambda i: (i, 0, 0)),
        out_shape=jax.ShapeDtypeStruct((t // r, PER_GROUP, r), I32),
        compiler_params=_cparams(1),
        name="moe_slots",
    )(eidx, base)


def _row_dma_loops(copy, tm):
    def start(r, c):
        copy(r, 0).start()
        copy(r, 1).start()
        return c

    def wait(r, c):
        copy(r, 0).wait()
        copy(r, 1).wait()
        return c

    lax.fori_loop(0, tm, start, 0, unroll=DMA_UNROLL)
    lax.fori_loop(0, tm, wait, 0, unroll=DMA_UNROLL)


def _dispatch_kernel(fill_ref, dest_ref, x_ref, out_ref, zbuf, sem):
    n_slots = out_ref.shape[0]

    @pl.when(pl.program_id(0) == 0)
    def _():
        zbuf[...] = jnp.zeros_like(zbuf)
        starts = [fill_ref[e] for e in range(N_EXPERTS)]
        starts += [fill_ref[N_EXPERTS] + j * MOE_BLOCK for j in range(N_EXPERTS)]
        live = [s >= 0 for s in starts[:N_EXPERTS]] + [s < n_slots for s in starts[N_EXPERTS:]]

        def zcopy(s):
            return pltpu.make_async_copy(zbuf, out_ref.at[pl.ds(s, MOE_BLOCK)], sem)

        for s, ok in zip(starts, live):
            @pl.when(ok)
            def _(s=s):
                zcopy(s).start()
        for s, ok in zip(starts, live):
            @pl.when(ok)
            def _(s=s):
                zcopy(s).wait()

    _row_dma_loops(lambda r, k: pltpu.make_async_copy(x_ref.at[r], out_ref.at[dest_ref[0, k, r]], sem),
                   x_ref.shape[0])


def _dispatch(dest3, hrow, last_blk, n_slots):
    t = hrow.shape[0]
    r = ROW_TILE
    grid_spec = pltpu.PrefetchScalarGridSpec(
        num_scalar_prefetch=1,
        grid=(t // r,),
        in_specs=[pl.BlockSpec((1, PER_GROUP, r), lambda i, lb: (i, 0, 0), memory_space=pltpu.SMEM),
                  pl.BlockSpec((r, ROW_SUB, LANES), lambda i, lb: (i, 0, 0))],
        out_specs=pl.BlockSpec(memory_space=pl.ANY),
        scratch_shapes=[pltpu.VMEM((MOE_BLOCK, ROW_SUB, LANES), F32), pltpu.SemaphoreType.DMA],
    )
    return pl.pallas_call(
        _dispatch_kernel,
        grid_spec=grid_spec,
        out_shape=jax.ShapeDtypeStruct((n_slots, ROW_SUB, LANES), F32),
        compiler_params=_cparams(1, "arbitrary"),
        name="moe_dispatch",
    )(last_blk, dest3, hrow)


def _expert_kernel(be_ref, nb_ref, x_ref, nw_ref, w1_ref, w3_ref, w2_ref, out_ref, w1b, w3b, w2b):
    i = pl.program_id(0)
    prev = be_ref[jnp.maximum(i - 1, 0)]

    @pl.when((i == 0) | (be_ref[i] != prev))
    def _():
        w1b[...] = w1_ref[0].astype(BF16)
        w3b[...] = w3_ref[0].astype(BF16)
        w2b[...] = w2_ref[0].astype(BF16)

    @pl.when(i < nb_ref[0])
    def _():
        x = x_ref[...].reshape(MOE_BLOCK, D_MODEL)
        ms = jnp.mean(x * x, axis=-1, keepdims=True)
        xn = ((x * lax.rsqrt(ms + RMS_EPS)) * nw_ref[...]).astype(BF16)
        a = jnp.dot(xn, w1b[...], preferred_element_type=F32)
        b = jnp.dot(xn, w3b[...], preferred_element_type=F32)
        mid = (_silu(a) * b).astype(BF16)
        out_ref[...] = jnp.dot(mid, w2b[...], preferred_element_type=F32).reshape(out_ref.shape)

    @pl.when(i >= nb_ref[0])
    def _():
        out_ref[...] = jnp.zeros_like(out_ref)


def _experts(xin, block_e, n_used, nw, w1, w3, w2):
    n_slots = xin.shape[0]
    nb = n_slots // MOE_BLOCK
    d, f = w1.shape[1], w1.shape[2]
    rows = (MOE_BLOCK, ROW_SUB, LANES)
    grid_spec = pltpu.PrefetchScalarGridSpec(
        num_scalar_prefetch=2,
        grid=(nb,),
        in_specs=[pl.BlockSpec(rows, lambda i, be, nu: (jnp.minimum(i, nu[0] - 1), 0, 0)),
                  pl.BlockSpec((1, d), lambda i, be, nu: (0, 0)),
                  pl.BlockSpec((1, d, f), lambda i, be, nu: (be[i], 0, 0)),
                  pl.BlockSpec((1, d, f), lambda i, be, nu: (be[i], 0, 0)),
                  pl.BlockSpec((1, f, d), lambda i, be, nu: (be[i], 0, 0))],
        out_specs=pl.BlockSpec(rows, lambda i, be, nu: (i, 0, 0)),
        scratch_shapes=[pltpu.VMEM((d, f), BF16), pltpu.VMEM((d, f), BF16), pltpu.VMEM((f, d), BF16)],
    )
    return pl.pallas_call(
        _expert_kernel,
        grid_spec=grid_spec,
        out_shape=jax.ShapeDtypeStruct((n_slots, ROW_SUB, LANES), F32),
        compiler_params=_cparams(1, "arbitrary"),
        name="moe_experts",
    )(block_e, n_used, xin, nw.reshape(1, d), w1, w3, w2)


def _combine_kernel(dest_ref, h_ref, wts_ref, y_ref, nw_ref, *refs, n_out, na, final_norm):
    out_refs, (gbuf, sem) = refs[:n_out], refs[n_out:]
    tm = h_ref.shape[0]
    _row_dma_loops(lambda r, k: pltpu.make_async_copy(y_ref.at[dest_ref[0, k, r]], gbuf.at[k, r], sem), tm)
    wts = wts_ref[...]
    g0 = gbuf[0].reshape(tm, D_MODEL)
    g1 = gbuf[1].reshape(tm, D_MODEL)
    out = h_ref[...] + (wts[:, 0:1] * g0 + wts[:, 1:2] * g1)
    if final_norm:
        ms = jnp.mean(out * out, axis=-1, keepdims=True)
        out = (out * lax.rsqrt(ms + RMS_EPS)) * nw_ref[...]
    if n_out == 1:
        out_refs[0][...] = out
    else:
        @pl.when(pl.program_id(0) < na)
        def _():
            out_refs[0][...] = out

        @pl.when(pl.program_id(0) >= na)
        def _():
            out_refs[1][...] = out


def _combine(dest3, h, wts, yout, nw, final_norm, split_rows=None):
    t, d = h.shape
    r = ROW_TILE
    if split_rows is None:
        na = t // r
        out_specs = [pl.BlockSpec((r, d), lambda i: (i, 0))]
        out_shape = [jax.ShapeDtypeStruct((t, d), F32)]
    else:
        na = split_rows // r
        out_specs = [pl.BlockSpec((r, d), lambda i: (jnp.minimum(i, na - 1), 0)),
                     pl.BlockSpec((r, d), lambda i: (jnp.maximum(i - na, 0), 0))]
        out_shape = [jax.ShapeDtypeStruct((split_rows, d), F32), jax.ShapeDtypeStruct((t - split_rows, d), F32)]
    outs = pl.pallas_call(
        functools.partial(_combine_kernel, n_out=len(out_specs), na=na, final_norm=final_norm),
        grid=(t // r,),
        in_specs=[pl.BlockSpec((1, PER_GROUP, r), lambda i: (i, 0, 0), memory_space=pltpu.SMEM),
                  pl.BlockSpec((r, d), lambda i: (i, 0)),
                  pl.BlockSpec((r, LANES), lambda i: (i, 0)),
                  pl.BlockSpec(memory_space=pl.ANY),
                  pl.BlockSpec((1, d), lambda i: (0, 0))],
        out_specs=out_specs,
        out_shape=out_shape,
        scratch_shapes=[pltpu.VMEM((2, r, ROW_SUB, LANES), F32), pltpu.SemaphoreType.DMA],
        compiler_params=_cparams(1, "arbitrary"),
        name="moe_combine",
    )(dest3, h, wts, yout, nw.reshape(1, d))
    return outs[0] if split_rows is None else tuple(outs)


def _moe(h1, hrow, eidx, wts, hist, norm_w, w1, w3, w2, final_nw=None, split_rows=None):
    t = h1.shape[0]
    tile_counts = hist[:, :, 0].astype(I32)
    counts = jnp.sum(tile_counts, axis=0)
    padded = (counts + MOE_BLOCK - 1) // MOE_BLOCK * MOE_BLOCK
    pad_end = jnp.cumsum(padded)
    pad_start = pad_end - padded
    n_blocks = -(-(2 * t) // MOE_BLOCK) + N_EXPERTS
    n_used = (pad_end[-1] // MOE_BLOCK).astype(I32).reshape(1)
    blk = jnp.minimum(jnp.arange(n_blocks, dtype=I32), n_used[0] - 1) * MOE_BLOCK
    block_e = jnp.minimum(jnp.sum((pad_end[None, :] <= blk[:, None]).astype(I32), axis=1), N_EXPERTS - 1)
    last_blk = jnp.concatenate([jnp.where(counts > 0, pad_end - MOE_BLOCK, -1), pad_end[-1:]]).astype(I32)
    tile_base = pad_start[None, :] + jnp.cumsum(tile_counts, axis=0) - tile_counts
    base = jnp.broadcast_to(tile_base.astype(F32)[:, :, None], tile_base.shape + (LANES,))
    dest3 = _slots(eidx, base)
    xin = _dispatch(dest3, hrow, last_blk, n_blocks * MOE_BLOCK)
    yout = _experts(xin, block_e, n_used, norm_w, w1, w3, w2)
    nw = final_nw if final_nw is not None else norm_w
    return _combine(dest3, h1, wts, yout, nw, final_nw is not None, split_rows)


def _even_mixers(x, seq, p):
    wi = p["even_w_in"][0]
    z_end = SSD_INNER
    xbc_end = z_end + SSD_CONV_DIM
    dt_end = xbc_end + 2 * SSD_HEADS
    w0 = jnp.concatenate([wi[:, :xbc_end], wi[:, dt_end:], wi[:, xbc_end:dt_end],
                          jnp.zeros((D_MODEL, LANES - 2 * SSD_HEADS), F32)], axis=1).astype(BF16)
    z, xbc, xp, dt = _norm_proj(x, p["norm_mix"][0], w0, (SSD_INNER, SSD_CONV_DIM, POOL_WIDTH, LANES),
                                (BF16, BF16, BF16, F32))
    xs, bm, cm = _ssd_conv(xbc, p["ssd_conv_w"][0], p["ssd_conv_b"][0], seq)
    y_ssd = _ssd_scan(xs, bm, cm, z, dt, p["ssd_dt_bias"][0], p["ssd_A_log"][0], p["ssd_D"][0],
                      p["ssd_norm_w"][0], seq)
    y_pool = _pool_mixer(xp, p["pool_w"][0], p["pool_scale"][0], seq)
    return y_ssd, y_pool


def _odd_mixers(h, seq, p):
    wo = p["odd_w_in"][0]
    qkv0 = 3 * SC_WIDTH + Q_DIM

    def dup_heads(w):
        return jnp.repeat(w.reshape(D_MODEL, N_KV_HEADS, 1, HEAD_DIM), 2, axis=2).reshape(D_MODEL, 2 * KV_DIM)

    w1 = jnp.concatenate([wo[:, :qkv0], dup_heads(wo[:, qkv0:qkv0 + KV_DIM]),
                          dup_heads(wo[:, qkv0 + KV_DIM:])], axis=1).astype(BF16)
    gb, gc, xc, q, kd, vd = _norm_proj(
        h, p["norm_mix"][1], w1, (SC_WIDTH, SC_WIDTH, SC_WIDTH, Q_DIM, 2 * KV_DIM, 2 * KV_DIM), (BF16,) * 6,
        rope_scales=(None, None, None, HEAD_DIM ** -0.5 * LOG2E, 1.0, None), seq=seq)
    y_conv = _short_conv(gb, gc, xc, p["sc_conv_w"][0], seq)
    y_attn = _attention(q, kd, vd, p["attn_sinks"][0], seq)
    return y_conv, y_attn


def _trunk(x, seq, p):
    y_ssd, y_pool = _even_mixers(x, seq, p)
    h, hrow, eidx, wts, hist = _out_router(y_ssd, y_pool, x, p["even_w_out"][0], p["norm_ffn"][0],
                                           p["moe_w_group"][0], p["moe_b_group"][0], p["moe_w_expert"][0],
                                           p["moe_b_expert"][0])
    h = _moe(h, hrow, eidx, wts, hist, p["norm_ffn"][0], p["moe_w1"][0], p["moe_w3"][0], p["moe_w2"][0])
    y_conv, y_attn = _odd_mixers(h, seq, p)
    h, hrow, eidx, wts, hist = _out_router(y_conv, y_attn, h, p["odd_w_out"][0], p["norm_ffn"][1],
                                           p["moe_w_group"][1], p["moe_b_group"][1], p["moe_w_expert"][1],
                                           p["moe_b_expert"][1])
    return _moe(h, hrow, eidx, wts, hist, p["norm_ffn"][1], p["moe_w1"][1], p["moe_w3"][1], p["moe_w2"][1],
                final_nw=p["norm_final"], split_rows=seq[0])


def kernel(x_prompt, x_sample, norm_mix, norm_ffn, norm_final, even_w_in, ssd_conv_w, ssd_conv_b, ssd_A_log,
           ssd_dt_bias, ssd_D, ssd_norm_w, pool_w, pool_scale, even_w_out, odd_w_in, sc_conv_w, attn_sinks,
           odd_w_out, moe_w_group, moe_b_group, moe_w_expert, moe_b_expert, moe_w1, moe_w3, moe_w2):
    p = dict(norm_mix=norm_mix, norm_ffn=norm_ffn, norm_final=norm_final, even_w_in=even_w_in,
             ssd_conv_w=ssd_conv_w, ssd_conv_b=ssd_conv_b, ssd_A_log=ssd_A_log, ssd_dt_bias=ssd_dt_bias,
             ssd_D=ssd_D, ssd_norm_w=ssd_norm_w, pool_w=pool_w, pool_scale=pool_scale, even_w_out=even_w_out,
             odd_w_in=odd_w_in, sc_conv_w=sc_conv_w, attn_sinks=attn_sinks, odd_w_out=odd_w_out,
             moe_w_group=moe_w_group, moe_b_group=moe_b_group, moe_w_expert=moe_w_expert,
             moe_b_expert=moe_b_expert, moe_w1=moe_w1, moe_w3=moe_w3, moe_w2=moe_w2)
    bp, lp, d = x_prompt.shape
    bs, ls, _ = x_sample.shape
    ta = bp * lp
    ya, yb = _trunk((x_prompt.reshape(ta, d), x_sample.reshape(bs * ls, d)), (ta, lp, ls), p)
    return ya.reshape(bp, lp, d), yb.reshape(bs, ls, d)
```

```python
import functools

import jax
import jax.numpy as jnp
from jax import lax
from jax.experimental import pallas as pl
from jax.experimental.pallas import tpu as pltpu

F32 = jnp.float32
BF16 = jnp.bfloat16
I32 = jnp.int32

RMS_EPS = 1e-6
D_MODEL = 1024
LANES = 128
ROW_SUB = D_MODEL // LANES
BF16_ROWS = 16
VMEM_LIMIT = 56 * 1024 * 1024

SSD_HEADS = 24
SSD_HEAD_DIM = 64
SSD_INNER = SSD_HEADS * SSD_HEAD_DIM
SSD_GROUPS = 4
SSD_STATE = 128
SSD_BC = SSD_GROUPS * SSD_STATE
SSD_CONV_DIM = SSD_INNER + 2 * SSD_BC
SSD_TAPS = 5
CHUNK = 128
POOL_WINDOWS = (2, 4, 8, 16)
POOL_WIDTH = 512
SC_WIDTH = 1024
N_Q_HEADS = 16
N_KV_HEADS = 4
HEAD_DIM = 64
Q_DIM = N_Q_HEADS * HEAD_DIM
KV_DIM = N_KV_HEADS * HEAD_DIM
WINDOW = 128
ROPE_THETA = 10000.0
N_GROUPS = 4
PER_GROUP = 8
N_EXPERTS = N_GROUPS * PER_GROUP
D_EXPERT = 512
MOE_BLOCK = 512
ROW_TILE = 512
DMA_UNROLL = 8


def _cparams(n_grid, semantics="parallel"):
    return pltpu.CompilerParams(dimension_semantics=(semantics,) * n_grid, vmem_limit_bytes=VMEM_LIMIT)


def _seq_pos(tok0, ta, la, lb):
    in_a = tok0 < ta
    seqlen = jnp.where(in_a, la, lb)
    pos = jnp.where(in_a, tok0 % la, (tok0 - ta) % lb)
    return pos, seqlen


def _row_parts(x):
    return tuple(x) if isinstance(x, (tuple, list)) else (x,)


def _row_in_specs(parts, r, w):
    if len(parts) == 1:
        return [pl.BlockSpec((r, w), lambda i: (i, 0))]
    na = parts[0].shape[0] // r
    return [pl.BlockSpec((r, w), lambda i: (jnp.minimum(i, na - 1), 0)),
            pl.BlockSpec((r, w), lambda i: (jnp.maximum(i - na, 0), 0))]


def _load_rows(refs, na):
    if len(refs) == 1:
        return refs[0][...]
    return jnp.where(pl.program_id(0) < na, refs[0][...], refs[1][...])


def _sigmoid(x):
    return 1.0 / (1.0 + jnp.exp(-x))


def _silu(x):
    return x * _sigmoid(x)


def _softplus(x):
    return jnp.maximum(x, 0.0) + jnp.log1p(jnp.exp(-jnp.abs(x)))


def _split3(x):
    hi = x.astype(BF16)
    r1 = x - hi.astype(F32)
    mid = r1.astype(BF16)
    lo = (r1 - mid.astype(F32)).astype(BF16)
    return hi, mid, lo


def _tri_dot(mask, v):
    m = jnp.where(mask, 1.0, 0.0).astype(BF16)
    hi, mid, lo = _split3(v)
    acc = jnp.dot(m, hi, preferred_element_type=F32)
    acc += jnp.dot(m, mid, preferred_element_type=F32)
    acc += jnp.dot(m, lo, preferred_element_type=F32)
    return acc


def _tn_dot(a, b):
    at = a.astype(F32).T.astype(BF16)
    return jnp.dot(at, b, preferred_element_type=F32)


def _rope(t, cos, sin_signed):
    n = t.shape[1]
    lane = lax.broadcasted_iota(I32, t.shape, 1)
    first = (lane % HEAD_DIM) < HEAD_DIM // 2
    rot = jnp.where(first, pltpu.roll(t, n - HEAD_DIM // 2, 1), pltpu.roll(t, HEAD_DIM // 2, 1))
    reps = n // LANES
    return t * jnp.tile(cos, (1, reps)) + rot * jnp.tile(sin_signed, (1, reps))


def _norm_proj_kernel(*refs, n_x, na, widths, rope_scales):
    x_refs, (nw_ref, w_ref), refs = refs[:n_x], refs[n_x:n_x + 2], refs[n_x + 2:]
    use_rope = any(s is not None for s in rope_scales)
    if use_rope:
        cos_ref, sin_ref = refs[:2]
        out_refs = refs[2:]
    else:
        out_refs = refs
    x = _load_rows(x_refs, na)
    ms = jnp.mean(x * x, axis=-1, keepdims=True)
    xn = ((x * lax.rsqrt(ms + RMS_EPS)) * nw_ref[...]).astype(BF16)
    off = 0
    for wd, o, rs in zip(widths, out_refs, rope_scales):
        for c0 in range(0, wd, 512):
            cw = min(512, wd - c0)
            v = jnp.dot(xn, w_ref[:, off + c0:off + c0 + cw], preferred_element_type=F32)
            if rs is not None:
                v = _rope(v, cos_ref[...], sin_ref[...]) * rs
            o[:, c0:c0 + cw] = v.astype(o.dtype)
        off += wd


def _norm_proj(x, nw, w, widths, dtypes, rope_scales=None, seq=None):
    parts = _row_parts(x)
    d = parts[0].shape[1]
    t = sum(p.shape[0] for p in parts)
    n = w.shape[1]
    assert sum(widths) == n and all(p.shape[0] % ROW_TILE == 0 for p in parts)
    rope_scales = tuple(rope_scales) if rope_scales is not None else (None,) * len(widths)
    in_specs = _row_in_specs(parts, ROW_TILE, d) + [pl.BlockSpec((1, d), lambda i: (0, 0)),
                                                    pl.BlockSpec((d, n), lambda i: (0, 0))]
    args = list(parts) + [nw.reshape(1, d), w]
    if any(s is not None for s in rope_scales):
        ta, la, lb = seq
        cos, sin = _rope_tables(max(la, lb))
        tab = pl.BlockSpec((ROW_TILE, LANES), lambda i: (_seq_pos(i * ROW_TILE, ta, la, lb)[0] // ROW_TILE, 0))
        in_specs += [tab, tab]
        args += [cos, sin]
    return pl.pallas_call(
        functools.partial(_norm_proj_kernel, n_x=len(parts), na=parts[0].shape[0] // ROW_TILE,
                          widths=tuple(widths), rope_scales=rope_scales),
        grid=(t // ROW_TILE,),
        in_specs=in_specs,
        out_specs=[pl.BlockSpec((ROW_TILE, wd), lambda i: (i, 0)) for wd in widths],
        out_shape=[jax.ShapeDtypeStruct((t, wd), dt) for wd, dt in zip(widths, dtypes)],
        compiler_params=_cparams(1),
        name="norm_proj",
    )(*args)


def _halo_specs(r, width, t):
    hb = r // BF16_ROWS
    last = t // BF16_ROWS - 1
    return [pl.BlockSpec((BF16_ROWS, width), lambda i: (jnp.maximum(i * hb - 1, 0), 0)),
            pl.BlockSpec((r, width), lambda i: (i, 0)),
            pl.BlockSpec((BF16_ROWS, width), lambda i: (jnp.minimum((i + 1) * hb, last), 0))]


def _fill_halo_buf(buf, prev_ref, cur_ref, next_ref, r, has_prev, has_next):
    buf[0:BF16_ROWS, :] = jnp.where(has_prev, prev_ref[...].astype(F32), 0.0)
    buf[BF16_ROWS:BF16_ROWS + r, :] = cur_ref[...].astype(F32)
    buf[BF16_ROWS + r:2 * BF16_ROWS + r, :] = jnp.where(has_next, next_ref[...].astype(F32), 0.0)


def _ssd_conv_kernel(prev_ref, cur_ref, next_ref, w_ref, b_ref, xs_ref, bm_ref, cm_ref, buf,
                     *, r, ta, la, lb):
    i = pl.program_id(0)
    pos, seqlen = _seq_pos(i * r, ta, la, lb)
    _fill_halo_buf(buf, prev_ref, cur_ref, next_ref, r, pos > 0, pos + r < seqlen)
    half = SSD_TAPS // 2
    for c0 in range(0, SSD_CONV_DIM, 512):
        for r0 in range(0, r, 128):
            acc = jnp.broadcast_to(b_ref[:, c0:c0 + 512], (128, 512))
            for k in range(SSD_TAPS):
                start = BF16_ROWS + r0 + k - half
                acc = acc + w_ref[k:k + 1, c0:c0 + 512] * buf[start:start + 128, c0:c0 + 512]
            y = _silu(acc).astype(BF16)
            if c0 < SSD_INNER:
                xs_ref[r0:r0 + 128, c0:c0 + 512] = y
            elif c0 < SSD_INNER + SSD_BC:
                bm_ref[r0:r0 + 128, :] = y
            else:
                cm_ref[r0:r0 + 128, :] = y


def _ssd_conv(xbc, conv_w, conv_b, seq):
    t = xbc.shape[0]
    r = ROW_TILE
    return pl.pallas_call(
        functools.partial(_ssd_conv_kernel, r=r, ta=seq[0], la=seq[1], lb=seq[2]),
        grid=(t // r,),
        in_specs=_halo_specs(r, SSD_CONV_DIM, t) + [
            pl.BlockSpec((SSD_TAPS, SSD_CONV_DIM), lambda i: (0, 0)),
            pl.BlockSpec((1, SSD_CONV_DIM), lambda i: (0, 0))],
        out_specs=[pl.BlockSpec((r, SSD_INNER), lambda i: (i, 0)),
                   pl.BlockSpec((r, SSD_BC), lambda i: (i, 0)),
                   pl.BlockSpec((r, SSD_BC), lambda i: (i, 0))],
        out_shape=[jax.ShapeDtypeStruct((t, SSD_INNER), BF16),
                   jax.ShapeDtypeStruct((t, SSD_BC), BF16),
                   jax.ShapeDtypeStruct((t, SSD_BC), BF16)],
        scratch_shapes=[pltpu.VMEM((r + 2 * BF16_ROWS, SSD_CONV_DIM), F32)],
        compiler_params=_cparams(1),
        name="ssd_conv",
    )(xbc, xbc, xbc, conv_w, conv_b.reshape(1, SSD_CONV_DIM))


def _chunk_iotas():
    ri = lax.broadcasted_iota(I32, (CHUNK, CHUNK), 0)
    ci = lax.broadcasted_iota(I32, (CHUNK, CHUNK), 1)
    return ri, ci


def _col(v, lane):
    return jnp.broadcast_to(v[:, lane:lane + 1], (CHUNK, LANES))


def _pair(left, a, b):
    return jnp.where(left, a, b)


def _ssd_prep_kernel(dt_ref, bias_ref, alog_ref, acs_ref, sfx_ref, wc_ref, rows_ref):
    ri, ci = _chunk_iotas()
    fwd_lane = ci < SSD_HEADS
    a_row = -jnp.exp(alog_ref[...])
    for k in range(dt_ref.shape[0] // CHUNK):
        rs = slice(k * CHUNK, (k + 1) * CHUNK)
        dt_all = _softplus(dt_ref[rs, :] + bias_ref[...])
        da = dt_all * a_row
        acs = _tri_dot(ci <= ri, da)
        sfx = _tri_dot(ci >= ri, da)
        acs_ref[rs, :] = acs
        sfx_ref[rs, :] = sfx
        wc_ref[rs, :] = dt_all * jnp.exp(jnp.where(fwd_lane, acs[CHUNK - 1:CHUNK, :] - acs, sfx[0:1, :] - sfx))
        rows_ref[k, 0] = acs.T
        rows_ref[k, 1] = sfx.T
        rows_ref[k, 2] = dt_all.T


def _ssd_bwd_kernel(x_ref, b_ref, c_ref, sfx_ref, wc_ref, out_ref, h_ref, xw_ref, *, n_chunks, ta, la, lb):
    c = n_chunks - 1 - pl.program_id(0)
    pos, seqlen = _seq_pos(c * CHUNK, ta, la, lb)

    @pl.when(pos + CHUNK == seqlen)
    def _():
        h_ref[...] = jnp.zeros_like(h_ref)

    _, ci = _chunk_iotas()
    left = ci < SSD_HEAD_DIM
    sfx = sfx_ref[...]
    wc = wc_ref[...]
    for g in range(SSD_GROUPS):
        gs = slice(g * 384, (g + 1) * 384)
        cg = c_ref[:, g * SSD_STATE:(g + 1) * SSD_STATE]
        bg = b_ref[:, g * SSD_STATE:(g + 1) * SSD_STATE]
        hg = h_ref[:, gs]
        yoff = jnp.dot(cg, hg.astype(BF16), preferred_element_type=F32)
        cds = []
        for jj in range(3):
            j = g * 3 + jj
            ls = slice(j * LANES, (j + 1) * LANES)
            h0 = SSD_HEADS + 2 * j
            es = jnp.exp(_pair(left, _col(sfx, h0), _col(sfx, h0 + 1)))
            wt = _pair(left, _col(wc, h0), _col(wc, h0 + 1))
            xw_ref[:, ls] = (x_ref[:, ls].astype(F32) * wt).astype(BF16)
            out_ref[:, ls] = (yoff[:, jj * LANES:(jj + 1) * LANES] * es).astype(out_ref.dtype)
            cds.append(es[0:1, :])
        cd = jnp.concatenate(cds, axis=1)
        h_ref[:, gs] = hg * cd + _tn_dot(bg, xw_ref[:, gs])


def _ssd_main_kernel(x_ref, b_ref, c_ref, z_ref, ybo_ref, acs_ref, sfx_ref, wc_ref, rows_ref, dskip_ref,
                     gw_ref, out_ref, h_ref, xw_ref, y_ref, *, ta, la, lb):
    pos, _ = _seq_pos(pl.program_id(0) * CHUNK, ta, la, lb)

    @pl.when(pos == 0)
    def _():
        h_ref[...] = jnp.zeros_like(h_ref)

    ri, ci = _chunk_iotas()
    left = ci < SSD_HEAD_DIM
    lower = ci <= ri
    low_s = ci < ri
    up_s = ci > ri
    acs = acs_ref[...]
    sfx = sfx_ref[...]
    wc = wc_ref[...]

    def row(k, h):
        return jnp.broadcast_to(rows_ref[0, k, h:h + 1, :], (CHUNK, CHUNK))

    for g in range(SSD_GROUPS):
        gs = slice(g * 384, (g + 1) * 384)
        cg = c_ref[:, g * SSD_STATE:(g + 1) * SSD_STATE]
        bg = b_ref[:, g * SSD_STATE:(g + 1) * SSD_STATE]
        cb = lax.dot_general(cg, bg, (((1,), (1,)), ((), ())), preferred_element_type=F32)
        hg = h_ref[:, gs]
        yoff = jnp.dot(cg, hg.astype(BF16), preferred_element_type=F32)
        cds = []
        for jj in range(3):
            j = g * 3 + jj
            ls = slice(j * LANES, (j + 1) * LANES)
            xt = x_ref[:, ls]
            xf = xt.astype(F32)
            yd = jnp.zeros((CHUNK, LANES), F32)
            cols = []
            for s in range(2):
                h = 2 * j + s
                hb = SSD_HEADS + h
                colf = _col(acs, h)
                cols.append(colf)
                arg = jnp.where(lower, colf - row(0, h), _col(sfx, hb) - row(1, hb))
                dtf = row(2, h)
                dtb = row(2, hb)
                dsel = jnp.where(low_s, dtf, jnp.where(up_s, dtb, dtf + dtb))
                m = (cb * jnp.exp(arg) * dsel).astype(BF16)
                xm = jnp.where(left if s == 0 else jnp.logical_not(left), xf, 0.0).astype(BF16)
                yd = yd + jnp.dot(m, xm, preferred_element_type=F32)
            es = jnp.exp(_pair(left, cols[0], cols[1]))
            wt = _pair(left, _col(wc, 2 * j), _col(wc, 2 * j + 1))
            xw_ref[:, ls] = (xf * wt).astype(BF16)
            y_ref[:, ls] = (yd + yoff[:, jj * LANES:(jj + 1) * LANES] * es
                            + ybo_ref[:, ls].astype(F32) + xf * dskip_ref[:, ls])
            cds.append(es[CHUNK - 1:CHUNK, :])
        cd = jnp.concatenate(cds, axis=1)
        h_ref[:, gs] = hg * cd + _tn_dot(bg, xw_ref[:, gs])

    y = y_ref[...] * _silu(z_ref[...].astype(F32))
    ms = jnp.mean(y * y, axis=-1, keepdims=True)
    out_ref[...] = ((y * lax.rsqrt(ms + RMS_EPS)) * gw_ref[...]).astype(out_ref.dtype)


def _ssd_scan(xs, bm, cm, z, dt, dt_bias, a_log, d_skip, gnorm_w, seq):
    t = xs.shape[0]
    n = t // CHUNK
    ta, la, lb = seq
    pad = LANES - 2 * SSD_HEADS
    bias = jnp.pad(dt_bias.reshape(1, 2 * SSD_HEADS), ((0, 0), (0, pad)))
    alog = jnp.pad(a_log.reshape(1, 2 * SSD_HEADS), ((0, 0), (0, pad)))
    dexp = jnp.repeat(d_skip, SSD_HEAD_DIM).reshape(1, SSD_INNER)
    small = lambda w: pl.BlockSpec((1, w), lambda i: (0, 0))

    pc = ROW_TILE // CHUNK
    lane_rows = pl.BlockSpec((ROW_TILE, LANES), lambda i: (i, 0))
    acs, sfx, wc, rows = pl.pallas_call(
        _ssd_prep_kernel,
        grid=(t // ROW_TILE,),
        in_specs=[lane_rows, small(LANES), small(LANES)],
        out_specs=[lane_rows, lane_rows, lane_rows,
                   pl.BlockSpec((pc, 3, CHUNK, CHUNK), lambda i: (i, 0, 0, 0))],
        out_shape=[jax.ShapeDtypeStruct((t, LANES), F32)] * 3 + [jax.ShapeDtypeStruct((n, 3, CHUNK, CHUNK), F32)],
        compiler_params=_cparams(1),
        name="ssd_prep",
    )(dt, bias, alog)

    rev = lambda i: (n - 1 - i, 0)
    ybo = pl.pallas_call(
        functools.partial(_ssd_bwd_kernel, n_chunks=n, ta=ta, la=la, lb=lb),
        grid=(n,),
        in_specs=[pl.BlockSpec((CHUNK, SSD_INNER), rev),
                  pl.BlockSpec((CHUNK, SSD_BC), rev),
                  pl.BlockSpec((CHUNK, SSD_BC), rev),
                  pl.BlockSpec((CHUNK, LANES), rev),
                  pl.BlockSpec((CHUNK, LANES), rev)],
        out_specs=pl.BlockSpec((CHUNK, SSD_INNER), rev),
        out_shape=jax.ShapeDtypeStruct((t, SSD_INNER), BF16),
        scratch_shapes=[pltpu.VMEM((SSD_STATE, SSD_INNER), F32),
                        pltpu.VMEM((CHUNK, SSD_INNER), BF16)],
        compiler_params=_cparams(1, "arbitrary"),
        name="ssd_bwd",
    )(xs, bm, cm, sfx, wc)

    fwd = lambda i: (i, 0)
    lane_chunk = pl.BlockSpec((CHUNK, LANES), fwd)
    return pl.pallas_call(
        functools.partial(_ssd_main_kernel, ta=ta, la=la, lb=lb),
        grid=(n,),
        in_specs=[pl.BlockSpec((CHUNK, SSD_INNER), fwd),
                  pl.BlockSpec((CHUNK, SSD_BC), fwd),
                  pl.BlockSpec((CHUNK, SSD_BC), fwd),
                  pl.BlockSpec((CHUNK, SSD_INNER), fwd),
                  pl.BlockSpec((CHUNK, SSD_INNER), fwd),
                  lane_chunk, lane_chunk, lane_chunk,
                  pl.BlockSpec((1, 3, CHUNK, CHUNK), lambda i: (i, 0, 0, 0)),
                  small(SSD_INNER), small(SSD_INNER)],
        out_specs=pl.BlockSpec((CHUNK, SSD_INNER), fwd),
        out_shape=jax.ShapeDtypeStruct((t, SSD_INNER), BF16),
        scratch_shapes=[pltpu.VMEM((SSD_STATE, SSD_INNER), F32),
                        pltpu.VMEM((CHUNK, SSD_INNER), BF16),
                        pltpu.VMEM((CHUNK, SSD_INNER), F32)],
        compiler_params=_cparams(1, "arbitrary"),
        name="ssd_main",
    )(xs, bm, cm, z, ybo, acs, sfx, wc, rows, dexp, gnorm_w.reshape(1, SSD_INNER))


def _pool_kernel(prev_ref, cur_ref, next_ref, w_ref, sc_ref, out_ref, buf, *, r, ta, la, lb):
    i = pl.program_id(0)
    pos, seqlen = _seq_pos(i * r, ta, la, lb)
    _fill_halo_buf(buf, prev_ref, cur_ref, next_ref, r, pos > 0, pos + r < seqlen)
    for gi, w in enumerate(POOL_WINDOWS):
        ls = slice(gi * LANES, (gi + 1) * LANES)
        for r0 in range(0, r, 128):
            base = BF16_ROWS + r0
            acc = buf[base - w // 2:base - w // 2 + 128, ls]
            for k in range(1, w):
                s = base - w // 2 + k
                acc = acc + buf[s:s + 128, ls]
            tpos = pos + r0 + lax.broadcasted_iota(I32, (128, LANES), 0)
            lo = jnp.maximum(tpos - w // 2, 0)
            hi = jnp.minimum(tpos - w // 2 + w, seqlen)
            mean = acc / (hi - lo).astype(F32)
            diff = (mean - buf[base:base + 128, ls]).astype(BF16)
            o = jnp.dot(diff, w_ref[gi], preferred_element_type=F32) * sc_ref[:, ls]
            out_ref[r0:r0 + 128, ls] = o.astype(out_ref.dtype)


def _pool_mixer(xp, pool_w, pool_scale, seq):
    t = xp.shape[0]
    r = ROW_TILE
    return pl.pallas_call(
        functools.partial(_pool_kernel, r=r, ta=seq[0], la=seq[1], lb=seq[2]),
        grid=(t // r,),
        in_specs=_halo_specs(r, POOL_WIDTH, t) + [
            pl.BlockSpec((len(POOL_WINDOWS), LANES, LANES), lambda i: (0, 0, 0)),
            pl.BlockSpec((1, POOL_WIDTH), lambda i: (0, 0))],
        out_specs=pl.BlockSpec((r, POOL_WIDTH), lambda i: (i, 0)),
        out_shape=jax.ShapeDtypeStruct((t, POOL_WIDTH), BF16),
        scratch_shapes=[pltpu.VMEM((r + 2 * BF16_ROWS, POOL_WIDTH), F32)],
        compiler_params=_cparams(1),
        name="pool_mixer",
    )(xp, xp, xp, pool_w.astype(BF16), pool_scale.reshape(1, POOL_WIDTH))


def _short_conv_kernel(gcp, gcc, gcn, xp, xc, xn, gb_ref, w_ref, out_ref, buf, *, r, ta, la, lb):
    i = pl.program_id(0)
    pos, seqlen = _seq_pos(i * r, ta, la, lb)
    has_prev = pos > 0
    has_next = pos + r < seqlen
    prod = lambda a, b: a[...].astype(F32) * b[...].astype(F32)
    buf[0:BF16_ROWS, :] = jnp.where(has_prev, prod(gcp, xp), 0.0)
    buf[BF16_ROWS:BF16_ROWS + r, :] = prod(gcc, xc)
    buf[BF16_ROWS + r:2 * BF16_ROWS + r, :] = jnp.where(has_next, prod(gcn, xn), 0.0)
    for c0 in range(0, SC_WIDTH, 512):
        cs = slice(c0, c0 + 512)
        for r0 in range(0, r, 128):
            base = BF16_ROWS + r0
            acc = w_ref[0:1, cs] * buf[base - 1:base + 127, cs]
            acc = acc + w_ref[1:2, cs] * buf[base:base + 128, cs]
            acc = acc + w_ref[2:3, cs] * buf[base + 1:base + 129, cs]
            out_ref[r0:r0 + 128, cs] = (gb_ref[r0:r0 + 128, cs].astype(F32) * acc).astype(out_ref.dtype)


def _short_conv(gb, gc, xc, conv_w, seq):
    t = gb.shape[0]
    r = ROW_TILE
    halo = _halo_specs(r, SC_WIDTH, t)
    return pl.pallas_call(
        functools.partial(_short_conv_kernel, r=r, ta=seq[0], la=seq[1], lb=seq[2]),
        grid=(t // r,),
        in_specs=halo + halo + [pl.BlockSpec((r, SC_WIDTH), lambda i: (i, 0)),
                                pl.BlockSpec((3, SC_WIDTH), lambda i: (0, 0))],
        out_specs=pl.BlockSpec((r, SC_WIDTH), lambda i: (i, 0)),
        out_shape=jax.ShapeDtypeStruct((t, SC_WIDTH), BF16),
        scratch_shapes=[pltpu.VMEM((r + 2 * BF16_ROWS, SC_WIDTH), F32)],
        compiler_params=_cparams(1),
        name="short_conv",
    )(gc, gc, gc, xc, xc, xc, gb, conv_w)


LOG2E = 1.4426950408889634


def _attn_kernel(q_ref, kp_ref, kc_ref, kn_ref, vp_ref, vc_ref, vn_ref, sink_ref, out_ref, *, ta, la, lb):
    pos, seqlen = _seq_pos(pl.program_id(0) * CHUNK, ta, la, lb)
    has_prev = pos > 0
    has_next = pos + CHUNK < seqlen
    rows = 2 * CHUNK
    qi = lax.broadcasted_iota(I32, (rows, 3 * CHUNK), 0) % CHUNK
    kk = lax.broadcasted_iota(I32, (rows, 3 * CHUNK), 1)
    mask = (kk >= qi) & (kk <= qi + 2 * WINDOW)
    mask = mask & ((kk >= CHUNK) | has_prev) & ((kk < 2 * CHUNK) | has_next)
    bias = jnp.where(mask, 0.0, -jnp.inf)
    lo_q = lax.broadcasted_iota(I32, (rows, LANES), 1) < HEAD_DIM
    lo_v = lax.broadcasted_iota(I32, (3 * CHUNK, LANES), 1) < HEAD_DIM
    top = lax.broadcasted_iota(I32, (rows, 1), 0) < CHUNK
    nt = (((1,), (1,)), ((), ()))
    zq = jnp.zeros((rows, LANES), BF16)
    zv = jnp.zeros((3 * CHUNK, LANES), BF16)
    for g in range(N_KV_HEADS):
        ls = slice(g * LANES, (g + 1) * LANES)
        kd = jnp.concatenate([kp_ref[:, ls], kc_ref[:, ls], kn_ref[:, ls]], axis=0)
        vd = jnp.concatenate([vp_ref[:, ls], vc_ref[:, ls], vn_ref[:, ls]], axis=0)
        q2 = jnp.concatenate([q_ref[:, (2 * g) * LANES:(2 * g + 1) * LANES],
                              q_ref[:, (2 * g + 1) * LANES:(2 * g + 2) * LANES]], axis=0)
        acc = jnp.zeros((rows, LANES), F32)
        for half in range(2):
            sel_q = lo_q if half == 0 else jnp.logical_not(lo_q)
            sel_v = lo_v if half == 0 else jnp.logical_not(lo_v)
            ha = 4 * g + half
            sink = jnp.where(top, sink_ref[:, ha:ha + 1], sink_ref[:, ha + 2:ha + 3]) * LOG2E
            s = lax.dot_general(jnp.where(sel_q, q2, zq), kd, nt, preferred_element_type=F32) + bias
            m = jnp.maximum(jnp.max(s, axis=-1, keepdims=True), sink)
            p = jnp.exp2(s - m)
            den = jnp.sum(p, axis=-1, keepdims=True) + jnp.exp2(sink - m)
            o = jnp.dot(p.astype(BF16), jnp.where(sel_v, vd, zv), preferred_element_type=F32)
            acc = acc + o * (1.0 / den)
        out_ref[:, (2 * g) * LANES:(2 * g + 1) * LANES] = acc[0:CHUNK].astype(out_ref.dtype)
        out_ref[:, (2 * g + 1) * LANES:(2 * g + 2) * LANES] = acc[CHUNK:rows].astype(out_ref.dtype)


def _rope_tables(lmax):
    inv = 1.0 / (ROPE_THETA ** (jnp.arange(0, HEAD_DIM, 2, dtype=F32) / HEAD_DIM))
    ang = jnp.arange(lmax, dtype=F32)[:, None] * inv[None, :]
    cos, sin = jnp.cos(ang), jnp.sin(ang)
    cos = jnp.concatenate([cos, cos, cos, cos], axis=1)
    sin = jnp.concatenate([-sin, sin, -sin, sin], axis=1)
    return cos, sin


def _attention(q, kd, vd, sinks, seq):
    t = q.shape[0]
    n = t // CHUNK
    ta, la, lb = seq
    nlast = n - 1
    cur = lambda i: (i, 0)
    prv = lambda i: (jnp.maximum(i - 1, 0), 0)
    nxt = lambda i: (jnp.minimum(i + 1, nlast), 0)
    kv = lambda f: pl.BlockSpec((CHUNK, 2 * KV_DIM), f)
    return pl.pallas_call(
        functools.partial(_attn_kernel, ta=ta, la=la, lb=lb),
        grid=(n,),
        in_specs=[pl.BlockSpec((CHUNK, Q_DIM), cur), kv(prv), kv(cur), kv(nxt), kv(prv), kv(cur), kv(nxt),
                  pl.BlockSpec((1, N_Q_HEADS), lambda i: (0, 0))],
        out_specs=pl.BlockSpec((CHUNK, Q_DIM), cur),
        out_shape=jax.ShapeDtypeStruct((t, Q_DIM), BF16),
        compiler_params=_cparams(1),
        name="band_attention",
    )(q, kd, kd, kd, vd, vd, vd, sinks.reshape(1, N_Q_HEADS))


def _out_router_kernel(a_ref, b_ref, *refs, n_h, na):
    h_refs, refs = refs[:n_h], refs[n_h:]
    wa_ref, wb_ref, nw_ref, wr_ref, br_ref, h_out, hrow_out, eidx_out, wts_out, hist_out = refs
    h1 = (_load_rows(h_refs, na) + jnp.dot(a_ref[...], wa_ref[...], preferred_element_type=F32)
          + jnp.dot(b_ref[...], wb_ref[...], preferred_element_type=F32))
    h_out[...] = h1
    hrow_out[...] = h1.reshape(hrow_out.shape)
    ms = jnp.mean(h1 * h1, axis=-1, keepdims=True)
    tn = (h1 * lax.rsqrt(ms + RMS_EPS)) * nw_ref[...]
    t_hi, t_mid, _ = _split3(tn)
    w_hi, w_mid, _ = _split3(wr_ref[...])
    nt = (((1,), (1,)), ((), ()))
    logits = (lax.dot_general(w_hi, t_hi, nt, preferred_element_type=F32)
              + lax.dot_general(w_hi, t_mid, nt, preferred_element_type=F32)
              + lax.dot_general(w_mid, t_hi, nt, preferred_element_type=F32)) + br_ref[...]
    tm = logits.shape[1]
    row = lax.broadcasted_iota(I32, (PER_GROUP, tm), 0)
    gl = jnp.where(row < N_GROUPS, logits[0:PER_GROUP], -jnp.inf)
    gmax = jnp.max(gl, axis=0, keepdims=True)
    gsel = jnp.min(jnp.where(gl == gmax, row, PER_GROUP), axis=0, keepdims=True)
    p_group = 1.0 / jnp.sum(jnp.exp(gl - gmax), axis=0, keepdims=True)
    el = logits[PER_GROUP:2 * PER_GROUP]
    for gi in range(1, N_GROUPS):
        el = jnp.where(gsel == gi, logits[(gi + 1) * PER_GROUP:(gi + 2) * PER_GROUP], el)
    m1 = jnp.max(el, axis=0, keepdims=True)
    i1 = jnp.min(jnp.where(el == m1, row, PER_GROUP), axis=0, keepdims=True)
    el2 = jnp.where(row == i1, -jnp.inf, el)
    m2 = jnp.max(el2, axis=0, keepdims=True)
    i2 = jnp.min(jnp.where(el2 == m2, row, PER_GROUP), axis=0, keepdims=True)
    ratio = jnp.exp(m2 - m1)
    w1 = p_group / (1.0 + ratio)
    w2 = p_group * ratio / (1.0 + ratio)
    ex1 = gsel * PER_GROUP + i1
    ex2 = gsel * PER_GROUP + i2
    eidx_out[...] = jnp.where(row == 0, ex1, jnp.where(row == 1, ex2, 0))
    erow = lax.broadcasted_iota(I32, (N_EXPERTS, tm), 0)
    hits = jnp.where((erow == ex1) | (erow == ex2), 1.0, 0.0)
    hist_out[0] = jnp.broadcast_to(jnp.sum(hits, axis=1, keepdims=True), (N_EXPERTS, LANES))
    wrow = lax.broadcasted_iota(I32, (LANES, tm), 0)
    wmat = jnp.where(wrow == 0, w1, jnp.where(wrow == 1, w2, 0.0))
    wts_out[...] = wmat.T


def _out_router(a, b, h, w_out, nw, w_group, b_group, w_expert, b_expert):
    h_parts = _row_parts(h)
    d = h_parts[0].shape[1]
    t = a.shape[0]
    ka, kb = a.shape[1], b.shape[1]
    wa = w_out[:ka].astype(BF16)
    wb = w_out[ka:].astype(BF16)
    wr = jnp.zeros((LANES, d), F32).at[0:N_GROUPS].set(w_group.T).at[PER_GROUP:PER_GROUP + N_EXPERTS].set(w_expert.T)
    br = jnp.zeros((LANES, 1), F32).at[0:N_GROUPS, 0].set(b_group).at[PER_GROUP:PER_GROUP + N_EXPERTS, 0].set(b_expert)
    r = ROW_TILE
    row_spec = lambda w: pl.BlockSpec((r, w), lambda i: (i, 0))
    const = lambda s: pl.BlockSpec(s, lambda i: (0, 0))
    return pl.pallas_call(
        functools.partial(_out_router_kernel, n_h=len(h_parts), na=h_parts[0].shape[0] // r),
        grid=(t // r,),
        in_specs=[row_spec(ka), row_spec(kb)] + _row_in_specs(h_parts, r, d) + [
            const((ka, d)), const((kb, d)), const((1, d)), const((LANES, d)), const((LANES, 1))],
        out_specs=[row_spec(d), pl.BlockSpec((r, ROW_SUB, LANES), lambda i: (i, 0, 0)),
                   pl.BlockSpec((PER_GROUP, r), lambda i: (0, i)), row_spec(LANES),
                   pl.BlockSpec((1, N_EXPERTS, LANES), lambda i: (i, 0, 0))],
        out_shape=[jax.ShapeDtypeStruct((t, d), F32), jax.ShapeDtypeStruct((t, ROW_SUB, LANES), F32),
                   jax.ShapeDtypeStruct((PER_GROUP, t), I32), jax.ShapeDtypeStruct((t, LANES), F32),
                   jax.ShapeDtypeStruct((t // r, N_EXPERTS, LANES), F32)],
        compiler_params=_cparams(1),
        name="out_proj_router",
    )(a, b, *h_parts, wa, wb, nw.reshape(1, d), wr, br)


def _slot_kernel(eidx_ref, base_ref, dest_out):
    tm = eidx_ref.shape[1]
    erow = lax.broadcasted_iota(I32, (N_EXPERTS, tm), 0)
    ti = lax.broadcasted_iota(I32, (tm, tm), 0)
    tj = lax.broadcasted_iota(I32, (tm, tm), 1)
    before = jnp.where(ti < tj, 1.0, 0.0).astype(BF16)
    base = base_ref[0][:, 0:1]
    slots = []
    for k in range(2):
        oh = jnp.where(eidx_ref[k:k + 1, :] == erow, 1.0, 0.0)
        prefix = jnp.dot(oh.astype(BF16), before, preferred_element_type=F32)
        slots.append(jnp.sum(oh * (prefix + base), axis=0, keepdims=True))
        base = base + jnp.sum(oh, axis=1, keepdims=True)
    row = lax.broadcasted_iota(I32, (PER_GROUP, tm), 0)
    dest_out[0] = jnp.where(row == 0, slots[0], jnp.where(row == 1, slots[1], 0.0)).astype(I32)


def _slots(eidx, base):
    t = eidx.shape[1]
    r = ROW_TILE
    return pl.pallas_call(
        _slot_kernel,
        grid=(t // r,),
        in_specs=[pl.BlockSpec((PER_GROUP, r), lambda i: (0, i)),
                  pl.BlockSpec((1, N_EXPERTS, LANES), lambda i: (i, 0, 0))],
        out_specs=pl.BlockSpec((1, PER_GROUP, r), lambda i: (i, 0, 0)),
        out_shape=jax.ShapeDtypeStruct((t // r, PER_GROUP, r), I32),
        compiler_params=_cparams(1),
        name="moe_slots",
    )(eidx, base)


def _row_dma_loops(copy, tm):
    def start(r, c):
        copy(r, 0).start(priority=0)
        copy(r, 1).start(priority=1)
        return c

    def wait(r, c):
        copy(r, 0).wait()
        copy(r, 1).wait()
        return c

    lax.fori_loop(0, tm, start, 0, unroll=DMA_UNROLL)
    lax.fori_loop(0, tm, wait, 0, unroll=DMA_UNROLL)


def _dispatch_kernel(fill_ref, dest_ref, x_ref, out_ref, zbuf, sem):
    n_slots = out_ref.shape[0]

    @pl.when(pl.program_id(0) == 0)
    def _():
        zbuf[...] = jnp.zeros_like(zbuf)
        starts = [fill_ref[e] for e in range(N_EXPERTS)]
        starts += [fill_ref[N_EXPERTS] + j * MOE_BLOCK for j in range(N_EXPERTS)]
        live = [s >= 0 for s in starts[:N_EXPERTS]] + [s < n_slots for s in starts[N_EXPERTS:]]

        def zcopy(s):
            return pltpu.make_async_copy(zbuf, out_ref.at[pl.ds(s, MOE_BLOCK)], sem)

        for s, ok in zip(starts, live):
            @pl.when(ok)
            def _(s=s):
                zcopy(s).start()
        for s, ok in zip(starts, live):
            @pl.when(ok)
            def _(s=s):
                zcopy(s).wait()

    _row_dma_loops(lambda r, k: pltpu.make_async_copy(x_ref.at[r], out_ref.at[dest_ref[0, k, r]], sem),
                   x_ref.shape[0])


def _dispatch(dest3, hrow, last_blk, n_slots):
    t = hrow.shape[0]
    r = ROW_TILE
    grid_spec = pltpu.PrefetchScalarGridSpec(
        num_scalar_prefetch=1,
        grid=(t // r,),
        in_specs=[pl.BlockSpec((1, PER_GROUP, r), lambda i, lb: (i, 0, 0), memory_space=pltpu.SMEM),
                  pl.BlockSpec((r, ROW_SUB, LANES), lambda i, lb: (i, 0, 0))],
        out_specs=pl.BlockSpec(memory_space=pl.ANY),
        scratch_shapes=[pltpu.VMEM((MOE_BLOCK, ROW_SUB, LANES), F32), pltpu.SemaphoreType.DMA],
    )
    return pl.pallas_call(
        _dispatch_kernel,
        grid_spec=grid_spec,
        out_shape=jax.ShapeDtypeStruct((n_slots, ROW_SUB, LANES), F32),
        compiler_params=_cparams(1, "arbitrary"),
        name="moe_dispatch",
    )(last_blk, dest3, hrow)


def _expert_kernel(be_ref, nb_ref, x_ref, nw_ref, w1_ref, w3_ref, w2_ref, out_ref, w1b, w3b, w2b):
    i = pl.program_id(0)
    prev = be_ref[jnp.maximum(i - 1, 0)]

    @pl.when((i == 0) | (be_ref[i] != prev))
    def _():
        w1b[...] = w1_ref[0, 0].astype(BF16)
        w3b[...] = w3_ref[0, 0].astype(BF16)
        w2b[...] = w2_ref[0, 0].astype(BF16)

    @pl.when(i < nb_ref[0])
    def _():
        x = x_ref[...].reshape(MOE_BLOCK, D_MODEL)
        ms = jnp.mean(x * x, axis=-1, keepdims=True)
        xn = ((x * lax.rsqrt(ms + RMS_EPS)) * nw_ref[...]).astype(BF16)
        a = jnp.dot(xn, w1b[...], preferred_element_type=F32)
        b = jnp.dot(xn, w3b[...], preferred_element_type=F32)
        mid = (_silu(a) * b).astype(BF16)
        out_ref[...] = jnp.dot(mid, w2b[...], preferred_element_type=F32).reshape(out_ref.shape)

    @pl.when(i >= nb_ref[0])
    def _():
        out_ref[...] = jnp.zeros_like(out_ref)


def _experts(xin, block_e, n_used, nw, w1, w3, w2, layer):
    n_slots = xin.shape[0]
    nb = n_slots // MOE_BLOCK
    d, f = w1.shape[2], w1.shape[3]
    rows = (MOE_BLOCK, ROW_SUB, LANES)
    grid_spec = pltpu.PrefetchScalarGridSpec(
        num_scalar_prefetch=2,
        grid=(nb,),
        in_specs=[pl.BlockSpec(rows, lambda i, be, nu: (jnp.minimum(i, nu[0] - 1), 0, 0)),
                  pl.BlockSpec((1, d), lambda i, be, nu: (0, 0)),
                  pl.BlockSpec((1, 1, d, f), lambda i, be, nu: (layer, be[i], 0, 0)),
                  pl.BlockSpec((1, 1, d, f), lambda i, be, nu: (layer, be[i], 0, 0)),
                  pl.BlockSpec((1, 1, f, d), lambda i, be, nu: (layer, be[i], 0, 0))],
        out_specs=pl.BlockSpec(rows, lambda i, be, nu: (i, 0, 0)),
        scratch_shapes=[pltpu.VMEM((d, f), BF16), pltpu.VMEM((d, f), BF16), pltpu.VMEM((f, d), BF16)],
    )
    return pl.pallas_call(
        _expert_kernel,
        grid_spec=grid_spec,
        out_shape=jax.ShapeDtypeStruct((n_slots, ROW_SUB, LANES), F32),
        compiler_params=_cparams(1, "arbitrary"),
        name="moe_experts",
    )(block_e, n_used, xin, nw.reshape(1, d), w1, w3, w2)


def _combine_kernel(dest_ref, h_ref, wts_ref, y_ref, nw_ref, *refs, n_out, na, final_norm):
    out_refs, (gbuf, sem) = refs[:n_out], refs[n_out:]
    tm = h_ref.shape[0]
    _row_dma_loops(lambda r, k: pltpu.make_async_copy(y_ref.at[dest_ref[0, k, r]], gbuf.at[k, r], sem), tm)
    wts = wts_ref[...]
    g0 = gbuf[0].reshape(tm, D_MODEL)
    g1 = gbuf[1].reshape(tm, D_MODEL)
    out = h_ref[...] + (wts[:, 0:1] * g0 + wts[:, 1:2] * g1)
    if final_norm:
        ms = jnp.mean(out * out, axis=-1, keepdims=True)
        out = (out * lax.rsqrt(ms + RMS_EPS)) * nw_ref[...]
    if n_out == 1:
        out_refs[0][...] = out
    else:
        @pl.when(pl.program_id(0) < na)
        def _():
            out_refs[0][...] = out

        @pl.when(pl.program_id(0) >= na)
        def _():
            out_refs[1][...] = out


def _combine(dest3, h, wts, yout, nw, final_norm, split_rows=None):
    t, d = h.shape
    r = ROW_TILE
    if split_rows is None:
        na = t // r
        out_specs = [pl.BlockSpec((r, d), lambda i: (i, 0))]
        out_shape = [jax.ShapeDtypeStruct((t, d), F32)]
    else:
        na = split_rows // r
        out_specs = [pl.BlockSpec((r, d), lambda i: (jnp.minimum(i, na - 1), 0)),
                     pl.BlockSpec((r, d), lambda i: (jnp.maximum(i - na, 0), 0))]
        out_shape = [jax.ShapeDtypeStruct((split_rows, d), F32), jax.ShapeDtypeStruct((t - split_rows, d), F32)]
    outs = pl.pallas_call(
        functools.partial(_combine_kernel, n_out=len(out_specs), na=na, final_norm=final_norm),
        grid=(t // r,),
        in_specs=[pl.BlockSpec((1, PER_GROUP, r), lambda i: (i, 0, 0), memory_space=pltpu.SMEM),
                  pl.BlockSpec((r, d), lambda i: (i, 0)),
                  pl.BlockSpec((r, LANES), lambda i: (i, 0)),
                  pl.BlockSpec(memory_space=pl.ANY),
                  pl.BlockSpec((1, d), lambda i: (0, 0))],
        out_specs=out_specs,
        out_shape=out_shape,
        scratch_shapes=[pltpu.VMEM((2, r, ROW_SUB, LANES), F32), pltpu.SemaphoreType.DMA],
        compiler_params=_cparams(1, "arbitrary"),
        name="moe_combine",
    )(dest3, h, wts, yout, nw.reshape(1, d))
    return outs[0] if split_rows is None else tuple(outs)


def _moe(h1, hrow, eidx, wts, hist, norm_w, w1, w3, w2, layer, final_nw=None, split_rows=None):
    t = h1.shape[0]
    tile_counts = hist[:, :, 0].astype(I32)
    counts = jnp.sum(tile_counts, axis=0)
    padded = (counts + MOE_BLOCK - 1) // MOE_BLOCK * MOE_BLOCK
    pad_end = jnp.cumsum(padded)
    pad_start = pad_end - padded
    n_blocks = -(-(2 * t) // MOE_BLOCK) + N_EXPERTS
    n_used = (pad_end[-1] // MOE_BLOCK).astype(I32).reshape(1)
    blk = jnp.minimum(jnp.arange(n_blocks, dtype=I32), n_used[0] - 1) * MOE_BLOCK
    block_e = jnp.minimum(jnp.sum((pad_end[None, :] <= blk[:, None]).astype(I32), axis=1), N_EXPERTS - 1)
    last_blk = jnp.concatenate([jnp.where(counts > 0, pad_end - MOE_BLOCK, -1), pad_end[-1:]]).astype(I32)
    tile_base = pad_start[None, :] + jnp.cumsum(tile_counts, axis=0) - tile_counts
    base = jnp.broadcast_to(tile_base.astype(F32)[:, :, None], tile_base.shape + (LANES,))
    dest3 = _slots(eidx, base)
    xin = _dispatch(dest3, hrow, last_blk, n_blocks * MOE_BLOCK)
    yout = _experts(xin, block_e, n_used, norm_w, w1, w3, w2, layer)
    nw = final_nw if final_nw is not None else norm_w
    return _combine(dest3, h1, wts, yout, nw, final_nw is not None, split_rows)


def _even_mixers(x, seq, p):
    wi = p["even_w_in"][0]
    z_end = SSD_INNER
    xbc_end = z_end + SSD_CONV_DIM
    dt_end = xbc_end + 2 * SSD_HEADS
    w0 = jnp.concatenate([wi[:, :xbc_end], wi[:, dt_end:], wi[:, xbc_end:dt_end],
                          jnp.zeros((D_MODEL, LANES - 2 * SSD_HEADS), F32)], axis=1).astype(BF16)
    z, xbc, xp, dt = _norm_proj(x, p["norm_mix"][0], w0, (SSD_INNER, SSD_CONV_DIM, POOL_WIDTH, LANES),
                                (BF16, BF16, BF16, F32))
    xs, bm, cm = _ssd_conv(xbc, p["ssd_conv_w"][0], p["ssd_conv_b"][0], seq)
    y_ssd = _ssd_scan(xs, bm, cm, z, dt, p["ssd_dt_bias"][0], p["ssd_A_log"][0], p["ssd_D"][0],
                      p["ssd_norm_w"][0], seq)
    y_pool = _pool_mixer(xp, p["pool_w"][0], p["pool_scale"][0], seq)
    return y_ssd, y_pool


def _odd_mixers(h, seq, p):
    wo = p["odd_w_in"][0]
    qkv0 = 3 * SC_WIDTH + Q_DIM

    def dup_heads(w):
        return jnp.repeat(w.reshape(D_MODEL, N_KV_HEADS, 1, HEAD_DIM), 2, axis=2).reshape(D_MODEL, 2 * KV_DIM)

    w1 = jnp.concatenate([wo[:, :qkv0], dup_heads(wo[:, qkv0:qkv0 + KV_DIM]),
                          dup_heads(wo[:, qkv0 + KV_DIM:])], axis=1).astype(BF16)
    gb, gc, xc, q, kd, vd = _norm_proj(
        h, p["norm_mix"][1], w1, (SC_WIDTH, SC_WIDTH, SC_WIDTH, Q_DIM, 2 * KV_DIM, 2 * KV_DIM), (BF16,) * 6,
        rope_scales=(None, None, None, HEAD_DIM ** -0.5 * LOG2E, 1.0, None), seq=seq)
    y_conv = _short_conv(gb, gc, xc, p["sc_conv_w"][0], seq)
    y_attn = _attention(q, kd, vd, p["attn_sinks"][0], seq)
    return y_conv, y_attn


def _trunk(x, seq, p):
    y_ssd, y_pool = _even_mixers(x, seq, p)
    h, hrow, eidx, wts, hist = _out_router(y_ssd, y_pool, x, p["even_w_out"][0], p["norm_ffn"][0],
                                           p["moe_w_group"][0], p["moe_b_group"][0], p["moe_w_expert"][0],
                                           p["moe_b_expert"][0])
    h = _moe(h, hrow, eidx, wts, hist, p["norm_ffn"][0], p["moe_w1"], p["moe_w3"], p["moe_w2"], 0)
    y_conv, y_attn = _odd_mixers(h, seq, p)
    h, hrow, eidx, wts, hist = _out_router(y_conv, y_attn, h, p["odd_w_out"][0], p["norm_ffn"][1],
                                           p["moe_w_group"][1], p["moe_b_group"][1], p["moe_w_expert"][1],
                                           p["moe_b_expert"][1])
    return _moe(h, hrow, eidx, wts, hist, p["norm_ffn"][1], p["moe_w1"], p["moe_w3"], p["moe_w2"], 1,
                final_nw=p["norm_final"], split_rows=seq[0])


def kernel(x_prompt, x_sample, norm_mix, norm_ffn, norm_final, even_w_in, ssd_conv_w, ssd_conv_b, ssd_A_log,
           ssd_dt_bias, ssd_D, ssd_norm_w, pool_w, pool_scale, even_w_out, odd_w_in, sc_conv_w, attn_sinks,
           odd_w_out, moe_w_group, moe_b_group, moe_w_expert, moe_b_expert, moe_w1, moe_w3, moe_w2):
    p = dict(norm_mix=norm_mix, norm_ffn=norm_ffn, norm_final=norm_final, even_w_in=even_w_in,
             ssd_conv_w=ssd_conv_w, ssd_conv_b=ssd_conv_b, ssd_A_log=ssd_A_log, ssd_dt_bias=ssd_dt_bias,
             ssd_D=ssd_D, ssd_norm_w=ssd_norm_w, pool_w=pool_w, pool_scale=pool_scale, even_w_out=even_w_out,
             odd_w_in=odd_w_in, sc_conv_w=sc_conv_w, attn_sinks=attn_sinks, odd_w_out=odd_w_out,
             moe_w_group=moe_w_group, moe_b_group=moe_b_group, moe_w_expert=moe_w_expert,
             moe_b_expert=moe_b_expert, moe_w1=moe_w1, moe_w3=moe_w3, moe_w2=moe_w2)
    bp, lp, d = x_prompt.shape
    bs, ls, _ = x_sample.shape
    ta = bp * lp
    ya, yb = _trunk((x_prompt.reshape(ta, d), x_sample.reshape(bs * ls, d)), (ta, lp, ls), p)
    return ya.reshape(bp, lp, d), yb.reshape(bs, ls, d)
```

```python
import functools

import jax
import jax.numpy as jnp
from jax import lax
from jax.experimental import pallas as pl
from jax.experimental.pallas import tpu as pltpu

F32 = jnp.float32
BF16 = jnp.bfloat16
I32 = jnp.int32

RMS_EPS = 1e-6
D_MODEL = 1024
LANES = 128
ROW_SUB = D_MODEL // LANES
BF16_ROWS = 16
VMEM_LIMIT = 56 * 1024 * 1024

SSD_HEADS = 24
SSD_HEAD_DIM = 64
SSD_INNER = SSD_HEADS * SSD_HEAD_DIM
SSD_GROUPS = 4
SSD_STATE = 128
SSD_BC = SSD_GROUPS * SSD_STATE
SSD_CONV_DIM = SSD_INNER + 2 * SSD_BC
SSD_TAPS = 5
CHUNK = 128
POOL_WINDOWS = (2, 4, 8, 16)
POOL_WIDTH = 512
SC_WIDTH = 1024
N_Q_HEADS = 16
N_KV_HEADS = 4
HEAD_DIM = 64
Q_DIM = N_Q_HEADS * HEAD_DIM
KV_DIM = N_KV_HEADS * HEAD_DIM
WINDOW = 128
ROPE_THETA = 10000.0
N_GROUPS = 4
PER_GROUP = 8
N_EXPERTS = N_GROUPS * PER_GROUP
D_EXPERT = 512
MOE_BLOCK = 512
ROW_TILE = 512
DMA_UNROLL = 8


def _cparams(n_grid, semantics="parallel"):
    return pltpu.CompilerParams(dimension_semantics=(semantics,) * n_grid, vmem_limit_bytes=VMEM_LIMIT)


def _seq_pos(tok0, ta, la, lb):
    in_a = tok0 < ta
    seqlen = jnp.where(in_a, la, lb)
    pos = jnp.where(in_a, tok0 % la, (tok0 - ta) % lb)
    return pos, seqlen


def _row_parts(x):
    return tuple(x) if isinstance(x, (tuple, list)) else (x,)


def _row_in_specs(parts, r, w):
    if len(parts) == 1:
        return [pl.BlockSpec((r, w), lambda i: (i, 0))]
    na = parts[0].shape[0] // r
    return [pl.BlockSpec((r, w), lambda i: (jnp.minimum(i, na - 1), 0)),
            pl.BlockSpec((r, w), lambda i: (jnp.maximum(i - na, 0), 0))]


def _load_rows(refs, na):
    if len(refs) == 1:
        return refs[0][...]
    return jnp.where(pl.program_id(0) < na, refs[0][...], refs[1][...])


def _sigmoid(x):
    return 1.0 / (1.0 + jnp.exp(-x))


def _silu(x):
    return x * _sigmoid(x)


def _softplus(x):
    return jnp.maximum(x, 0.0) + jnp.log1p(jnp.exp(-jnp.abs(x)))


def _split3(x):
    hi = x.astype(BF16)
    r1 = x - hi.astype(F32)
    mid = r1.astype(BF16)
    lo = (r1 - mid.astype(F32)).astype(BF16)
    return hi, mid, lo


def _tri_dot(mask, v):
    m = jnp.where(mask, 1.0, 0.0).astype(BF16)
    hi, mid, lo = _split3(v)
    acc = jnp.dot(m, hi, preferred_element_type=F32)
    acc += jnp.dot(m, mid, preferred_element_type=F32)
    acc += jnp.dot(m, lo, preferred_element_type=F32)
    return acc


def _tn_dot(a, b):
    at = a.astype(F32).T.astype(BF16)
    return jnp.dot(at, b, preferred_element_type=F32)


def _rope(t, cos, sin_signed):
    n = t.shape[1]
    lane = lax.broadcasted_iota(I32, t.shape, 1)
    first = (lane % HEAD_DIM) < HEAD_DIM // 2
    rot = jnp.where(first, pltpu.roll(t, n - HEAD_DIM // 2, 1), pltpu.roll(t, HEAD_DIM // 2, 1))
    reps = n // LANES
    return t * jnp.tile(cos, (1, reps)) + rot * jnp.tile(sin_signed, (1, reps))


def _norm_proj_kernel(*refs, n_x, na, widths, rope_scales):
    x_refs, (nw_ref, w_ref), refs = refs[:n_x], refs[n_x:n_x + 2], refs[n_x + 2:]
    use_rope = any(s is not None for s in rope_scales)
    if use_rope:
        cos_ref, sin_ref = refs[:2]
        out_refs = refs[2:]
    else:
        out_refs = refs
    x = _load_rows(x_refs, na)
    ms = jnp.mean(x * x, axis=-1, keepdims=True)
    xn = ((x * lax.rsqrt(ms + RMS_EPS)) * nw_ref[...]).astype(BF16)
    off = 0
    for wd, o, rs in zip(widths, out_refs, rope_scales):
        for c0 in range(0, wd, 512):
            cw = min(512, wd - c0)
            v = jnp.dot(xn, w_ref[:, off + c0:off + c0 + cw], preferred_element_type=F32)
            if rs is not None:
                v = _rope(v, cos_ref[...], sin_ref[...]) * rs
            o[:, c0:c0 + cw] = v.astype(o.dtype)
        off += wd


def _norm_proj(x, nw, w, widths, dtypes, rope_scales=None, seq=None):
    parts = _row_parts(x)
    d = parts[0].shape[1]
    t = sum(p.shape[0] for p in parts)
    n = w.shape[1]
    assert sum(widths) == n and all(p.shape[0] % ROW_TILE == 0 for p in parts)
    rope_scales = tuple(rope_scales) if rope_scales is not None else (None,) * len(widths)
    in_specs = _row_in_specs(parts, ROW_TILE, d) + [pl.BlockSpec((1, d), lambda i: (0, 0)),
                                                    pl.BlockSpec((d, n), lambda i: (0, 0))]
    args = list(parts) + [nw.reshape(1, d), w]
    if any(s is not None for s in rope_scales):
        ta, la, lb = seq
        cos, sin = _rope_tables(max(la, lb))
        tab = pl.BlockSpec((ROW_TILE, LANES), lambda i: (_seq_pos(i * ROW_TILE, ta, la, lb)[0] // ROW_TILE, 0))
        in_specs += [tab, tab]
        args += [cos, sin]
    return pl.pallas_call(
        functools.partial(_norm_proj_kernel, n_x=len(parts), na=parts[0].shape[0] // ROW_TILE,
                          widths=tuple(widths), rope_scales=rope_scales),
        grid=(t // ROW_TILE,),
        in_specs=in_specs,
        out_specs=[pl.BlockSpec((ROW_TILE, wd), lambda i: (i, 0)) for wd in widths],
        out_shape=[jax.ShapeDtypeStruct((t, wd), dt) for wd, dt in zip(widths, dtypes)],
        compiler_params=_cparams(1),
        name="norm_proj",
    )(*args)


def _halo_specs(r, width, t):
    hb = r // BF16_ROWS
    last = t // BF16_ROWS - 1
    return [pl.BlockSpec((BF16_ROWS, width), lambda i: (jnp.maximum(i * hb - 1, 0), 0)),
            pl.BlockSpec((r, width), lambda i: (i, 0)),
            pl.BlockSpec((BF16_ROWS, width), lambda i: (jnp.minimum((i + 1) * hb, last), 0))]


def _fill_halo_buf(buf, prev_ref, cur_ref, next_ref, r, has_prev, has_next):
    buf[0:BF16_ROWS, :] = jnp.where(has_prev, prev_ref[...].astype(F32), 0.0)
    buf[BF16_ROWS:BF16_ROWS + r, :] = cur_ref[...].astype(F32)
    buf[BF16_ROWS + r:2 * BF16_ROWS + r, :] = jnp.where(has_next, next_ref[...].astype(F32), 0.0)


def _ssd_conv_kernel(prev_ref, cur_ref, next_ref, w_ref, b_ref, xs_ref, bm_ref, cm_ref, buf,
                     *, r, ta, la, lb):
    i = pl.program_id(0)
    pos, seqlen = _seq_pos(i * r, ta, la, lb)
    _fill_halo_buf(buf, prev_ref, cur_ref, next_ref, r, pos > 0, pos + r < seqlen)
    half = SSD_TAPS // 2
    for c0 in range(0, SSD_CONV_DIM, 512):
        for r0 in range(0, r, 128):
            acc = jnp.broadcast_to(b_ref[:, c0:c0 + 512], (128, 512))
            for k in range(SSD_TAPS):
                start = BF16_ROWS + r0 + k - half
                acc = acc + w_ref[k:k + 1, c0:c0 + 512] * buf[start:start + 128, c0:c0 + 512]
            y = _silu(acc).astype(BF16)
            if c0 < SSD_INNER:
                xs_ref[r0:r0 + 128, c0:c0 + 512] = y
            elif c0 < SSD_INNER + SSD_BC:
                bm_ref[r0:r0 + 128, :] = y
            else:
                cm_ref[r0:r0 + 128, :] = y


def _ssd_conv(xbc, conv_w, conv_b, seq):
    t = xbc.shape[0]
    r = ROW_TILE
    return pl.pallas_call(
        functools.partial(_ssd_conv_kernel, r=r, ta=seq[0], la=seq[1], lb=seq[2]),
        grid=(t // r,),
        in_specs=_halo_specs(r, SSD_CONV_DIM, t) + [
            pl.BlockSpec((SSD_TAPS, SSD_CONV_DIM), lambda i: (0, 0)),
            pl.BlockSpec((1, SSD_CONV_DIM), lambda i: (0, 0))],
        out_specs=[pl.BlockSpec((r, SSD_INNER), lambda i: (i, 0)),
                   pl.BlockSpec((r, SSD_BC), lambda i: (i, 0)),
                   pl.BlockSpec((r, SSD_BC), lambda i: (i, 0))],
        out_shape=[jax.ShapeDtypeStruct((t, SSD_INNER), BF16),
                   jax.ShapeDtypeStruct((t, SSD_BC), BF16),
                   jax.ShapeDtypeStruct((t, SSD_BC), BF16)],
        scratch_shapes=[pltpu.VMEM((r + 2 * BF16_ROWS, SSD_CONV_DIM), F32)],
        compiler_params=_cparams(1),
        name="ssd_conv",
    )(xbc, xbc, xbc, conv_w, conv_b.reshape(1, SSD_CONV_DIM))


def _chunk_iotas():
    ri = lax.broadcasted_iota(I32, (CHUNK, CHUNK), 0)
    ci = lax.broadcasted_iota(I32, (CHUNK, CHUNK), 1)
    return ri, ci


def _col(v, lane):
    return jnp.broadcast_to(v[:, lane:lane + 1], (CHUNK, LANES))


def _pair(left, a, b):
    return jnp.where(left, a, b)


def _ssd_prep_kernel(dt_ref, bias_ref, alog_ref, acs_ref, sfx_ref, wc_ref, rows_ref):
    ri, ci = _chunk_iotas()
    fwd_lane = ci < SSD_HEADS
    a_row = -jnp.exp(alog_ref[...])
    for k in range(dt_ref.shape[0] // CHUNK):
        rs = slice(k * CHUNK, (k + 1) * CHUNK)
        dt_all = _softplus(dt_ref[rs, :] + bias_ref[...])
        da = dt_all * a_row
        acs = _tri_dot(ci <= ri, da)
        sfx = _tri_dot(ci >= ri, da)
        acs_ref[rs, :] = acs
        sfx_ref[rs, :] = sfx
        wc_ref[rs, :] = dt_all * jnp.exp(jnp.where(fwd_lane, acs[CHUNK - 1:CHUNK, :] - acs, sfx[0:1, :] - sfx))
        rows_ref[k, 0] = acs.T
        rows_ref[k, 1] = sfx.T
        rows_ref[k, 2] = dt_all.T


def _ssd_bwd_kernel(x_ref, b_ref, c_ref, sfx_ref, wc_ref, out_ref, h_ref, xw_ref, *, n_chunks, ta, la, lb):
    c = n_chunks - 1 - pl.program_id(0)
    pos, seqlen = _seq_pos(c * CHUNK, ta, la, lb)

    @pl.when(pos + CHUNK == seqlen)
    def _():
        h_ref[...] = jnp.zeros_like(h_ref)

    _, ci = _chunk_iotas()
    left = ci < SSD_HEAD_DIM
    sfx = sfx_ref[...]
    wc = wc_ref[...]
    for g in range(SSD_GROUPS):
        gs = slice(g * 384, (g + 1) * 384)
        cg = c_ref[:, g * SSD_STATE:(g + 1) * SSD_STATE]
        bg = b_ref[:, g * SSD_STATE:(g + 1) * SSD_STATE]
        hg = h_ref[:, gs]
        yoff = jnp.dot(cg, hg.astype(BF16), preferred_element_type=F32)
        cds = []
        for jj in range(3):
            j = g * 3 + jj
            ls = slice(j * LANES, (j + 1) * LANES)
            h0 = SSD_HEADS + 2 * j
            es = jnp.exp(_pair(left, _col(sfx, h0), _col(sfx, h0 + 1)))
            wt = _pair(left, _col(wc, h0), _col(wc, h0 + 1))
            xw_ref[:, ls] = (x_ref[:, ls].astype(F32) * wt).astype(BF16)
            out_ref[:, ls] = (yoff[:, jj * LANES:(jj + 1) * LANES] * es).astype(out_ref.dtype)
            cds.append(es[0:1, :])
        cd = jnp.concatenate(cds, axis=1)
        h_ref[:, gs] = hg * cd + _tn_dot(bg, xw_ref[:, gs])


def _ssd_main_kernel(x_ref, b_ref, c_ref, z_ref, ybo_ref, acs_ref, sfx_ref, wc_ref, rows_ref, dskip_ref,
                     gw_ref, out_ref, h_ref, xw_ref, y_ref, *, ta, la, lb):
    pos, _ = _seq_pos(pl.program_id(0) * CHUNK, ta, la, lb)

    @pl.when(pos == 0)
    def _():
        h_ref[...] = jnp.zeros_like(h_ref)

    ri, ci = _chunk_iotas()
    left = ci < SSD_HEAD_DIM
    lower = ci <= ri
    low_s = ci < ri
    up_s = ci > ri
    acs = acs_ref[...]
    sfx = sfx_ref[...]
    wc = wc_ref[...]

    def row(k, h):
        return jnp.broadcast_to(rows_ref[0, k, h:h + 1, :], (CHUNK, CHUNK))

    for g in range(SSD_GROUPS):
        gs = slice(g * 384, (g + 1) * 384)
        cg = c_ref[:, g * SSD_STATE:(g + 1) * SSD_STATE]
        bg = b_ref[:, g * SSD_STATE:(g + 1) * SSD_STATE]
        cb = lax.dot_general(cg, bg, (((1,), (1,)), ((), ())), preferred_element_type=F32)
        hg = h_ref[:, gs]
        yoff = jnp.dot(cg, hg.astype(BF16), preferred_element_type=F32)
        cds = []
        for jj in range(3):
            j = g * 3 + jj
            ls = slice(j * LANES, (j + 1) * LANES)
            xt = x_ref[:, ls]
            xf = xt.astype(F32)
            yd = jnp.zeros((CHUNK, LANES), F32)
            cols = []
            for s in range(2):
                h = 2 * j + s
                hb = SSD_HEADS + h
                colf = _col(acs, h)
                cols.append(colf)
                arg = jnp.where(lower, colf - row(0, h), _col(sfx, hb) - row(1, hb))
                dtf = row(2, h)
                dtb = row(2, hb)
                dsel = jnp.where(low_s, dtf, jnp.where(up_s, dtb, dtf + dtb))
                m = (cb * jnp.exp(arg) * dsel).astype(BF16)
                xm = jnp.where(left if s == 0 else jnp.logical_not(left), xf, 0.0).astype(BF16)
                yd = yd + jnp.dot(m, xm, preferred_element_type=F32)
            es = jnp.exp(_pair(left, cols[0], cols[1]))
            wt = _pair(left, _col(wc, 2 * j), _col(wc, 2 * j + 1))
            xw_ref[:, ls] = (xf * wt).astype(BF16)
            y_ref[:, ls] = (yd + yoff[:, jj * LANES:(jj + 1) * LANES] * es
                            + ybo_ref[:, ls].astype(F32) + xf * dskip_ref[:, ls])
            cds.append(es[CHUNK - 1:CHUNK, :])
        cd = jnp.concatenate(cds, axis=1)
        h_ref[:, gs] = hg * cd + _tn_dot(bg, xw_ref[:, gs])

    y = y_ref[...] * _silu(z_ref[...].astype(F32))
    ms = jnp.mean(y * y, axis=-1, keepdims=True)
    out_ref[...] = ((y * lax.rsqrt(ms + RMS_EPS)) * gw_ref[...]).astype(out_ref.dtype)


def _ssd_scan(xs, bm, cm, z, dt, dt_bias, a_log, d_skip, gnorm_w, seq):
    t = xs.shape[0]
    n = t // CHUNK
    ta, la, lb = seq
    pad = LANES - 2 * SSD_HEADS
    bias = jnp.pad(dt_bias.reshape(1, 2 * SSD_HEADS), ((0, 0), (0, pad)))
    alog = jnp.pad(a_log.reshape(1, 2 * SSD_HEADS), ((0, 0), (0, pad)))
    dexp = jnp.repeat(d_skip, SSD_HEAD_DIM).reshape(1, SSD_INNER)
    small = lambda w: pl.BlockSpec((1, w), lambda i: (0, 0))

    pc = ROW_TILE // CHUNK
    lane_rows = pl.BlockSpec((ROW_TILE, LANES), lambda i: (i, 0))
    acs, sfx, wc, rows = pl.pallas_call(
        _ssd_prep_kernel,
        grid=(t // ROW_TILE,),
        in_specs=[lane_rows, small(LANES), small(LANES)],
        out_specs=[lane_rows, lane_rows, lane_rows,
                   pl.BlockSpec((pc, 3, CHUNK, CHUNK), lambda i: (i, 0, 0, 0))],
        out_shape=[jax.ShapeDtypeStruct((t, LANES), F32)] * 3 + [jax.ShapeDtypeStruct((n, 3, CHUNK, CHUNK), F32)],
        compiler_params=_cparams(1),
        name="ssd_prep",
    )(dt, bias, alog)

    rev = lambda i: (n - 1 - i, 0)
    ybo = pl.pallas_call(
        functools.partial(_ssd_bwd_kernel, n_chunks=n, ta=ta, la=la, lb=lb),
        grid=(n,),
        in_specs=[pl.BlockSpec((CHUNK, SSD_INNER), rev),
                  pl.BlockSpec((CHUNK, SSD_BC), rev),
                  pl.BlockSpec((CHUNK, SSD_BC), rev),
                  pl.BlockSpec((CHUNK, LANES), rev),
                  pl.BlockSpec((CHUNK, LANES), rev)],
        out_specs=pl.BlockSpec((CHUNK, SSD_INNER), rev),
        out_shape=jax.ShapeDtypeStruct((t, SSD_INNER), BF16),
        scratch_shapes=[pltpu.VMEM((SSD_STATE, SSD_INNER), F32),
                        pltpu.VMEM((CHUNK, SSD_INNER), BF16)],
        compiler_params=_cparams(1, "arbitrary"),
        name="ssd_bwd",
    )(xs, bm, cm, sfx, wc)

    fwd = lambda i: (i, 0)
    lane_chunk = pl.BlockSpec((CHUNK, LANES), fwd)
    return pl.pallas_call(
        functools.partial(_ssd_main_kernel, ta=ta, la=la, lb=lb),
        grid=(n,),
        in_specs=[pl.BlockSpec((CHUNK, SSD_INNER), fwd),
                  pl.BlockSpec((CHUNK, SSD_BC), fwd),
                  pl.BlockSpec((CHUNK, SSD_BC), fwd),
                  pl.BlockSpec((CHUNK, SSD_INNER), fwd),
                  pl.BlockSpec((CHUNK, SSD_INNER), fwd),
                  lane_chunk, lane_chunk, lane_chunk,
                  pl.BlockSpec((1, 3, CHUNK, CHUNK), lambda i: (i, 0, 0, 0)),
                  small(SSD_INNER), small(SSD_INNER)],
        out_specs=pl.BlockSpec((CHUNK, SSD_INNER), fwd),
        out_shape=jax.ShapeDtypeStruct((t, SSD_INNER), BF16),
        scratch_shapes=[pltpu.VMEM((SSD_STATE, SSD_INNER), F32),
                        pltpu.VMEM((CHUNK, SSD_INNER), BF16),
                        pltpu.VMEM((CHUNK, SSD_INNER), F32)],
        compiler_params=_cparams(1, "arbitrary"),
        name="ssd_main",
    )(xs, bm, cm, z, ybo, acs, sfx, wc, rows, dexp, gnorm_w.reshape(1, SSD_INNER))


def _pool_kernel(prev_ref, cur_ref, next_ref, w_ref, sc_ref, out_ref, buf, *, r, ta, la, lb):
    i = pl.program_id(0)
    pos, seqlen = _seq_pos(i * r, ta, la, lb)
    _fill_halo_buf(buf, prev_ref, cur_ref, next_ref, r, pos > 0, pos + r < seqlen)
    for gi, w in enumerate(POOL_WINDOWS):
        ls = slice(gi * LANES, (gi + 1) * LANES)
        for r0 in range(0, r, 128):
            base = BF16_ROWS + r0
            acc = buf[base - w // 2:base - w // 2 + 128, ls]
            for k in range(1, w):
                s = base - w // 2 + k
                acc = acc + buf[s:s + 128, ls]
            tpos = pos + r0 + lax.broadcasted_iota(I32, (128, LANES), 0)
            lo = jnp.maximum(tpos - w // 2, 0)
            hi = jnp.minimum(tpos - w // 2 + w, seqlen)
            mean = acc / (hi - lo).astype(F32)
            diff = (mean - buf[base:base + 128, ls]).astype(BF16)
            o = jnp.dot(diff, w_ref[gi], preferred_element_type=F32) * sc_ref[:, ls]
            out_ref[r0:r0 + 128, ls] = o.astype(out_ref.dtype)


def _pool_mixer(xp, pool_w, pool_scale, seq):
    t = xp.shape[0]
    r = ROW_TILE
    return pl.pallas_call(
        functools.partial(_pool_kernel, r=r, ta=seq[0], la=seq[1], lb=seq[2]),
        grid=(t // r,),
        in_specs=_halo_specs(r, POOL_WIDTH, t) + [
            pl.BlockSpec((len(POOL_WINDOWS), LANES, LANES), lambda i: (0, 0, 0)),
            pl.BlockSpec((1, POOL_WIDTH), lambda i: (0, 0))],
        out_specs=pl.BlockSpec((r, POOL_WIDTH), lambda i: (i, 0)),
        out_shape=jax.ShapeDtypeStruct((t, POOL_WIDTH), BF16),
        scratch_shapes=[pltpu.VMEM((r + 2 * BF16_ROWS, POOL_WIDTH), F32)],
        compiler_params=_cparams(1),
        name="pool_mixer",
    )(xp, xp, xp, pool_w.astype(BF16), pool_scale.reshape(1, POOL_WIDTH))


def _short_conv_kernel(gcp, gcc, gcn, xp, xc, xn, gb_ref, w_ref, out_ref, buf, *, r, ta, la, lb):
    i = pl.program_id(0)
    pos, seqlen = _seq_pos(i * r, ta, la, lb)
    has_prev = pos > 0
    has_next = pos + r < seqlen
    prod = lambda a, b: a[...].astype(F32) * b[...].astype(F32)
    buf[0:BF16_ROWS, :] = jnp.where(has_prev, prod(gcp, xp), 0.0)
    buf[BF16_ROWS:BF16_ROWS + r, :] = prod(gcc, xc)
    buf[BF16_ROWS + r:2 * BF16_ROWS + r, :] = jnp.where(has_next, prod(gcn, xn), 0.0)
    for c0 in range(0, SC_WIDTH, 512):
        cs = slice(c0, c0 + 512)
        for r0 in range(0, r, 128):
            base = BF16_ROWS + r0
            acc = w_ref[0:1, cs] * buf[base - 1:base + 127, cs]
            acc = acc + w_ref[1:2, cs] * buf[base:base + 128, cs]
            acc = acc + w_ref[2:3, cs] * buf[base + 1:base + 129, cs]
            out_ref[r0:r0 + 128, cs] = (gb_ref[r0:r0 + 128, cs].astype(F32) * acc).astype(out_ref.dtype)


def _short_conv(gb, gc, xc, conv_w, seq):
    t = gb.shape[0]
    r = ROW_TILE
    halo = _halo_specs(r, SC_WIDTH, t)
    return pl.pallas_call(
        functools.partial(_short_conv_kernel, r=r, ta=seq[0], la=seq[1], lb=seq[2]),
        grid=(t // r,),
        in_specs=halo + halo + [pl.BlockSpec((r, SC_WIDTH), lambda i: (i, 0)),
                                pl.BlockSpec((3, SC_WIDTH), lambda i: (0, 0))],
        out_specs=pl.BlockSpec((r, SC_WIDTH), lambda i: (i, 0)),
        out_shape=jax.ShapeDtypeStruct((t, SC_WIDTH), BF16),
        scratch_shapes=[pltpu.VMEM((r + 2 * BF16_ROWS, SC_WIDTH), F32)],
        compiler_params=_cparams(1),
        name="short_conv",
    )(gc, gc, gc, xc, xc, xc, gb, conv_w)


LOG2E = 1.4426950408889634


def _attn_kernel(q_ref, kp_ref, kc_ref, kn_ref, vp_ref, vc_ref, vn_ref, sink_ref, out_ref, *, ta, la, lb):
    pos, seqlen = _seq_pos(pl.program_id(0) * CHUNK, ta, la, lb)
    has_prev = pos > 0
    has_next = pos + CHUNK < seqlen
    rows = 4 * CHUNK
    qi = lax.broadcasted_iota(I32, (rows, 3 * CHUNK), 0) % CHUNK
    kk = lax.broadcasted_iota(I32, (rows, 3 * CHUNK), 1)
    mask = (kk >= qi) & (kk <= qi + 2 * WINDOW)
    mask = mask & ((kk >= CHUNK) | has_prev) & ((kk < 2 * CHUNK) | has_next)
    bias = jnp.where(mask, 0.0, -jnp.inf)
    lo_q = lax.broadcasted_iota(I32, (2 * CHUNK, LANES), 1) < HEAD_DIM
    lo_v = lax.broadcasted_iota(I32, (3 * CHUNK, LANES), 1) < HEAD_DIM
    rowi = lax.broadcasted_iota(I32, (rows, 1), 0)
    nt = (((1,), (1,)), ((), ()))
    zq = jnp.zeros((2 * CHUNK, LANES), BF16)
    zv = jnp.zeros((3 * CHUNK, LANES), BF16)
    scores, sinks, vds = [], [], []
    for g in range(N_KV_HEADS):
        ls = slice(g * LANES, (g + 1) * LANES)
        kd = jnp.concatenate([kp_ref[:, ls], kc_ref[:, ls], kn_ref[:, ls]], axis=0)
        vds.append(jnp.concatenate([vp_ref[:, ls], vc_ref[:, ls], vn_ref[:, ls]], axis=0))
        q2 = jnp.concatenate([q_ref[:, (2 * g) * LANES:(2 * g + 1) * LANES],
                              q_ref[:, (2 * g + 1) * LANES:(2 * g + 2) * LANES]], axis=0)
        q4 = jnp.concatenate([jnp.where(lo_q, q2, zq), jnp.where(lo_q, zq, q2)], axis=0)
        scores.append(lax.dot_general(q4, kd, nt, preferred_element_type=F32) + bias)
        heads = (4 * g, 4 * g + 2, 4 * g + 1, 4 * g + 3)
        sk = sink_ref[:, heads[3]:heads[3] + 1]
        for k in (2, 1, 0):
            sk = jnp.where(rowi < (k + 1) * CHUNK, sink_ref[:, heads[k]:heads[k] + 1], sk)
        sinks.append(sk * LOG2E)
    s = jnp.concatenate(scores, axis=0)
    sink = jnp.concatenate(sinks, axis=0)
    m = jnp.maximum(jnp.max(s, axis=-1, keepdims=True), sink)
    p = jnp.exp2(s - m)
    inv = 1.0 / (jnp.sum(p, axis=-1, keepdims=True) + jnp.exp2(sink - m))
    p = p.astype(BF16)
    for g in range(N_KV_HEADS):
        pg = p[g * rows:(g + 1) * rows]
        ig = inv[g * rows:(g + 1) * rows]
        half = 2 * CHUNK
        acc = (jnp.dot(pg[0:half], jnp.where(lo_v, vds[g], zv), preferred_element_type=F32) * ig[0:half]
               + jnp.dot(pg[half:rows], jnp.where(lo_v, zv, vds[g]), preferred_element_type=F32) * ig[half:rows])
        out_ref[:, (2 * g) * LANES:(2 * g + 1) * LANES] = acc[0:CHUNK].astype(out_ref.dtype)
        out_ref[:, (2 * g + 1) * LANES:(2 * g + 2) * LANES] = acc[CHUNK:half].astype(out_ref.dtype)


def _rope_tables(lmax):
    inv = 1.0 / (ROPE_THETA ** (jnp.arange(0, HEAD_DIM, 2, dtype=F32) / HEAD_DIM))
    ang = jnp.arange(lmax, dtype=F32)[:, None] * inv[None, :]
    cos, sin = jnp.cos(ang), jnp.sin(ang)
    cos = jnp.concatenate([cos, cos, cos, cos], axis=1)
    sin = jnp.concatenate([-sin, sin, -sin, sin], axis=1)
    return cos, sin


def _attention(q, kd, vd, sinks, seq):
    t = q.shape[0]
    n = t // CHUNK
    ta, la, lb = seq
    nlast = n - 1
    cur = lambda i: (i, 0)
    prv = lambda i: (jnp.maximum(i - 1, 0), 0)
    nxt = lambda i: (jnp.minimum(i + 1, nlast), 0)
    kv = lambda f: pl.BlockSpec((CHUNK, 2 * KV_DIM), f)
    return pl.pallas_call(
        functools.partial(_attn_kernel, ta=ta, la=la, lb=lb),
        grid=(n,),
        in_specs=[pl.BlockSpec((CHUNK, Q_DIM), cur), kv(prv), kv(cur), kv(nxt), kv(prv), kv(cur), kv(nxt),
                  pl.BlockSpec((1, N_Q_HEADS), lambda i: (0, 0))],
        out_specs=pl.BlockSpec((CHUNK, Q_DIM), cur),
        out_shape=jax.ShapeDtypeStruct((t, Q_DIM), BF16),
        compiler_params=_cparams(1),
        name="band_attention",
    )(q, kd, kd, kd, vd, vd, vd, sinks.reshape(1, N_Q_HEADS))


def _out_router_kernel(a_ref, b_ref, *refs, n_h, na):
    h_refs, refs = refs[:n_h], refs[n_h:]
    wa_ref, wb_ref, nw_ref, wr_ref, br_ref, h_out, hrow_out, eidx_out, wts_out, hist_out = refs
    h1 = (_load_rows(h_refs, na) + jnp.dot(a_ref[...], wa_ref[...], preferred_element_type=F32)
          + jnp.dot(b_ref[...], wb_ref[...], preferred_element_type=F32))
    h_out[...] = h1
    hrow_out[...] = h1.reshape(hrow_out.shape)
    ms = jnp.mean(h1 * h1, axis=-1, keepdims=True)
    tn = (h1 * lax.rsqrt(ms + RMS_EPS)) * nw_ref[...]
    t_hi, t_mid, _ = _split3(tn)
    w_hi, w_mid, _ = _split3(wr_ref[...])
    nt = (((1,), (1,)), ((), ()))
    logits = (lax.dot_general(w_hi, t_hi, nt, preferred_element_type=F32)
              + lax.dot_general(w_hi, t_mid, nt, preferred_element_type=F32)
              + lax.dot_general(w_mid, t_hi, nt, preferred_element_type=F32)) + br_ref[...]
    tm = logits.shape[1]
    row = lax.broadcasted_iota(I32, (PER_GROUP, tm), 0)
    gl = jnp.where(row < N_GROUPS, logits[0:PER_GROUP], -jnp.inf)
    gmax = jnp.max(gl, axis=0, keepdims=True)
    gsel = jnp.min(jnp.where(gl == gmax, row, PER_GROUP), axis=0, keepdims=True)
    p_group = 1.0 / jnp.sum(jnp.exp(gl - gmax), axis=0, keepdims=True)
    el = logits[PER_GROUP:2 * PER_GROUP]
    for gi in range(1, N_GROUPS):
        el = jnp.where(gsel == gi, logits[(gi + 1) * PER_GROUP:(gi + 2) * PER_GROUP], el)
    m1 = jnp.max(el, axis=0, keepdims=True)
    i1 = jnp.min(jnp.where(el == m1, row, PER_GROUP), axis=0, keepdims=True)
    el2 = jnp.where(row == i1, -jnp.inf, el)
    m2 = jnp.max(el2, axis=0, keepdims=True)
    i2 = jnp.min(jnp.where(el2 == m2, row, PER_GROUP), axis=0, keepdims=True)
    ratio = jnp.exp(m2 - m1)
    w1 = p_group / (1.0 + ratio)
    w2 = p_group * ratio / (1.0 + ratio)
    ex1 = gsel * PER_GROUP + i1
    ex2 = gsel * PER_GROUP + i2
    eidx_out[...] = jnp.where(row == 0, ex1, jnp.where(row == 1, ex2, 0))
    erow = lax.broadcasted_iota(I32, (N_EXPERTS, tm), 0)
    hits = jnp.where((erow == ex1) | (erow == ex2), 1.0, 0.0)
    hist_out[0] = jnp.broadcast_to(jnp.sum(hits, axis=1, keepdims=True), (N_EXPERTS, LANES))
    wrow = lax.broadcasted_iota(I32, (LANES, tm), 0)
    wmat = jnp.where(wrow == 0, w1, jnp.where(wrow == 1, w2, 0.0))
    wts_out[...] = wmat.T


def _out_router(a, b, h, w_out, nw, w_group, b_group, w_expert, b_expert):
    h_parts = _row_parts(h)
    d = h_parts[0].shape[1]
    t = a.shape[0]
    ka, kb = a.shape[1], b.shape[1]
    wa = w_out[:ka].astype(BF16)
    wb = w_out[ka:].astype(BF16)
    wr = jnp.zeros((LANES, d), F32).at[0:N_GROUPS].set(w_group.T).at[PER_GROUP:PER_GROUP + N_EXPERTS].set(w_expert.T)
    br = jnp.zeros((LANES, 1), F32).at[0:N_GROUPS, 0].set(b_group).at[PER_GROUP:PER_GROUP + N_EXPERTS, 0].set(b_expert)
    r = ROW_TILE
    row_spec = lambda w: pl.BlockSpec((r, w), lambda i: (i, 0))
    const = lambda s: pl.BlockSpec(s, lambda i: (0, 0))
    return pl.pallas_call(
        functools.partial(_out_router_kernel, n_h=len(h_parts), na=h_parts[0].shape[0] // r),
        grid=(t // r,),
        in_specs=[row_spec(ka), row_spec(kb)] + _row_in_specs(h_parts, r, d) + [
            const((ka, d)), const((kb, d)), const((1, d)), const((LANES, d)), const((LANES, 1))],
        out_specs=[row_spec(d), pl.BlockSpec((r, ROW_SUB, LANES), lambda i: (i, 0, 0)),
                   pl.BlockSpec((PER_GROUP, r), lambda i: (0, i)), row_spec(LANES),
                   pl.BlockSpec((1, N_EXPERTS, LANES), lambda i: (i, 0, 0))],
        out_shape=[jax.ShapeDtypeStruct((t, d), F32), jax.ShapeDtypeStruct((t, ROW_SUB, LANES), F32),
                   jax.ShapeDtypeStruct((PER_GROUP, t), I32), jax.ShapeDtypeStruct((t, LANES), F32),
                   jax.ShapeDtypeStruct((t // r, N_EXPERTS, LANES), F32)],
        compiler_params=_cparams(1),
        name="out_proj_router",
    )(a, b, *h_parts, wa, wb, nw.reshape(1, d), wr, br)


def _slot_kernel(eidx_ref, base_ref, dest_out):
    tm = eidx_ref.shape[1]
    erow = lax.broadcasted_iota(I32, (N_EXPERTS, tm), 0)
    ti = lax.broadcasted_iota(I32, (tm, tm), 0)
    tj = lax.broadcasted_iota(I32, (tm, tm), 1)
    before = jnp.where(ti < tj, 1.0, 0.0).astype(BF16)
    base = base_ref[0][:, 0:1]
    slots = []
    for k in range(2):
        oh = jnp.where(eidx_ref[k:k + 1, :] == erow, 1.0, 0.0)
        prefix = jnp.dot(oh.astype(BF16), before, preferred_element_type=F32)
        slots.append(jnp.sum(oh * (prefix + base), axis=0, keepdims=True))
        base = base + jnp.sum(oh, axis=1, keepdims=True)
    row = lax.broadcasted_iota(I32, (PER_GROUP, tm), 0)
    dest_out[0] = jnp.where(row == 0, slots[0], jnp.where(row == 1, slots[1], 0.0)).astype(I32)


def _slots(eidx, base):
    t = eidx.shape[1]
    r = ROW_TILE
    return pl.pallas_call(
        _slot_kernel,
        grid=(t // r,),
        in_specs=[pl.BlockSpec((PER_GROUP, r), lambda i: (0, i)),
                  pl.BlockSpec((1, N_EXPERTS, LANES), lambda i: (i, 0, 0))],
        out_specs=pl.BlockSpec((1, PER_GROUP, r), lambda i: (i, 0, 0)),
        out_shape=jax.ShapeDtypeStruct((t // r, PER_GROUP, r), I32),
        compiler_params=_cparams(1),
        name="moe_slots",
    )(eidx, base)


def _row_dma_loops(copy, tm):
    def start(r, c):
        copy(r, 0).start(priority=0)
        copy(r, 1).start(priority=1)
        return c

    def wait(r, c):
        copy(r, 0).wait()
        copy(r, 1).wait()
        return c

    lax.fori_loop(0, tm, start, 0, unroll=DMA_UNROLL)
    lax.fori_loop(0, tm, wait, 0, unroll=DMA_UNROLL)


def _dispatch_kernel(fill_ref, dest_ref, x_ref, out_ref, zbuf, sem):
    n_slots = out_ref.shape[0]

    @pl.when(pl.program_id(0) == 0)
    def _():
        zbuf[...] = jnp.zeros_like(zbuf)
        starts = [fill_ref[e] for e in range(N_EXPERTS)]
        starts += [fill_ref[N_EXPERTS] + j * MOE_BLOCK for j in range(N_EXPERTS)]
        live = [s >= 0 for s in starts[:N_EXPERTS]] + [s < n_slots for s in starts[N_EXPERTS:]]

        def zcopy(s):
            return pltpu.make_async_copy(zbuf, out_ref.at[pl.ds(s, MOE_BLOCK)], sem)

        for s, ok in zip(starts, live):
            @pl.when(ok)
            def _(s=s):
                zcopy(s).start()
        for s, ok in zip(starts, live):
            @pl.when(ok)
            def _(s=s):
                zcopy(s).wait()

    _row_dma_loops(lambda r, k: pltpu.make_async_copy(x_ref.at[r], out_ref.at[dest_ref[0, k, r]], sem),
                   x_ref.shape[0])


def _dispatch(dest3, hrow, last_blk, n_slots):
    t = hrow.shape[0]
    r = ROW_TILE
    grid_spec = pltpu.PrefetchScalarGridSpec(
        num_scalar_prefetch=1,
        grid=(t // r,),
        in_specs=[pl.BlockSpec((1, PER_GROUP, r), lambda i, lb: (i, 0, 0), memory_space=pltpu.SMEM),
                  pl.BlockSpec((r, ROW_SUB, LANES), lambda i, lb: (i, 0, 0))],
        out_specs=pl.BlockSpec(memory_space=pl.ANY),
        scratch_shapes=[pltpu.VMEM((MOE_BLOCK, ROW_SUB, LANES), F32), pltpu.SemaphoreType.DMA],
    )
    return pl.pallas_call(
        _dispatch_kernel,
        grid_spec=grid_spec,
        out_shape=jax.ShapeDtypeStruct((n_slots, ROW_SUB, LANES), F32),
        compiler_params=_cparams(1, "arbitrary"),
        name="moe_dispatch",
    )(last_blk, dest3, hrow)


def _expert_kernel(be_ref, nb_ref, x_ref, nw_ref, w1_ref, w3_ref, w2_ref, out_ref, w1b, w3b, w2b):
    i = pl.program_id(0)
    prev = be_ref[jnp.maximum(i - 1, 0)]

    @pl.when((i == 0) | (be_ref[i] != prev))
    def _():
        w1b[...] = w1_ref[0, 0].astype(BF16)
        w3b[...] = w3_ref[0, 0].astype(BF16)
        w2b[...] = w2_ref[0, 0].astype(BF16)

    @pl.when(i < nb_ref[0])
    def _():
        x = x_ref[...].reshape(MOE_BLOCK, D_MODEL)
        ms = jnp.mean(x * x, axis=-1, keepdims=True)
        xn = ((x * lax.rsqrt(ms + RMS_EPS)) * nw_ref[...]).astype(BF16)
        a = jnp.dot(xn, w1b[...], preferred_element_type=F32)
        b = jnp.dot(xn, w3b[...], preferred_element_type=F32)
        mid = (_silu(a) * b).astype(BF16)
        out_ref[...] = jnp.dot(mid, w2b[...], preferred_element_type=F32).reshape(out_ref.shape)

    @pl.when(i >= nb_ref[0])
    def _():
        out_ref[...] = jnp.zeros_like(out_ref)


def _experts(xin, block_e, n_used, nw, w1, w3, w2, layer):
    n_slots = xin.shape[0]
    nb = n_slots // MOE_BLOCK
    d, f = w1.shape[2], w1.shape[3]
    rows = (MOE_BLOCK, ROW_SUB, LANES)
    grid_spec = pltpu.PrefetchScalarGridSpec(
        num_scalar_prefetch=2,
        grid=(nb,),
        in_specs=[pl.BlockSpec(rows, lambda i, be, nu: (jnp.minimum(i, nu[0] - 1), 0, 0)),
                  pl.BlockSpec((1, d), lambda i, be, nu: (0, 0)),
                  pl.BlockSpec((1, 1, d, f), lambda i, be, nu: (layer, be[i], 0, 0)),
                  pl.BlockSpec((1, 1, d, f), lambda i, be, nu: (layer, be[i], 0, 0)),
                  pl.BlockSpec((1, 1, f, d), lambda i, be, nu: (layer, be[i], 0, 0))],
        out_specs=pl.BlockSpec(rows, lambda i, be, nu: (i, 0, 0)),
        scratch_shapes=[pltpu.VMEM((d, f), BF16), pltpu.VMEM((d, f), BF16), pltpu.VMEM((f, d), BF16)],
    )
    return pl.pallas_call(
        _expert_kernel,
        grid_spec=grid_spec,
        out_shape=jax.ShapeDtypeStruct((n_slots, ROW_SUB, LANES), F32),
        compiler_params=_cparams(1, "arbitrary"),
        name="moe_experts",
    )(block_e, n_used, xin, nw.reshape(1, d), w1, w3, w2)


def _combine_kernel(dest_ref, h_ref, wts_ref, y_ref, nw_ref, *refs, n_out, na, final_norm):
    out_refs, (gbuf, sem) = refs[:n_out], refs[n_out:]
    tm = h_ref.shape[0]
    _row_dma_loops(lambda r, k: pltpu.make_async_copy(y_ref.at[dest_ref[0, k, r]], gbuf.at[k, r], sem), tm)
    wts = wts_ref[...]
    g0 = gbuf[0].reshape(tm, D_MODEL)
    g1 = gbuf[1].reshape(tm, D_MODEL)
    out = h_ref[...] + (wts[:, 0:1] * g0 + wts[:, 1:2] * g1)
    if final_norm:
        ms = jnp.mean(out * out, axis=-1, keepdims=True)
        out = (out * lax.rsqrt(ms + RMS_EPS)) * nw_ref[...]
    if n_out == 1:
        out_refs[0][...] = out
    else:
        @pl.when(pl.program_id(0) < na)
        def _():
            out_refs[0][...] = out

        @pl.when(pl.program_id(0) >= na)
        def _():
            out_refs[1][...] = out


def _combine(dest3, h, wts, yout, nw, final_norm, split_rows=None):
    t, d = h.shape
    r = ROW_TILE
    if split_rows is None:
        na = t // r
        out_specs = [pl.BlockSpec((r, d), lambda i: (i, 0))]
        out_shape = [jax.ShapeDtypeStruct((t, d), F32)]
    else:
        na = split_rows // r
        out_specs = [pl.BlockSpec((r, d), lambda i: (jnp.minimum(i, na - 1), 0)),
                     pl.BlockSpec((r, d), lambda i: (jnp.maximum(i - na, 0), 0))]
        out_shape = [jax.ShapeDtypeStruct((split_rows, d), F32), jax.ShapeDtypeStruct((t - split_rows, d), F32)]
    outs = pl.pallas_call(
        functools.partial(_combine_kernel, n_out=len(out_specs), na=na, final_norm=final_norm),
        grid=(t // r,),
        in_specs=[pl.BlockSpec((1, PER_GROUP, r), lambda i: (i, 0, 0), memory_space=pltpu.SMEM),
                  pl.BlockSpec((r, d), lambda i: (i, 0)),
                  pl.BlockSpec((r, LANES), lambda i: (i, 0)),
                  pl.BlockSpec(memory_space=pl.ANY),
                  pl.BlockSpec((1, d), lambda i: (0, 0))],
        out_specs=out_specs,
        out_shape=out_shape,
        scratch_shapes=[pltpu.VMEM((2, r, ROW_SUB, LANES), F32), pltpu.SemaphoreType.DMA],
        compiler_params=_cparams(1, "arbitrary"),
        name="moe_combine",
    )(dest3, h, wts, yout, nw.reshape(1, d))
    return outs[0] if split_rows is None else tuple(outs)


def _moe(h1, hrow, eidx, wts, hist, norm_w, w1, w3, w2, layer, final_nw=None, split_rows=None):
    t = h1.shape[0]
    tile_counts = hist[:, :, 0].astype(I32)
    counts = jnp.sum(tile_counts, axis=0)
    padded = (counts + MOE_BLOCK - 1) // MOE_BLOCK * MOE_BLOCK
    pad_end = jnp.cumsum(padded)
    pad_start = pad_end - padded
    n_blocks = -(-(2 * t) // MOE_BLOCK) + N_EXPERTS
    n_used = (pad_end[-1] // MOE_BLOCK).astype(I32).reshape(1)
    blk = jnp.minimum(jnp.arange(n_blocks, dtype=I32), n_used[0] - 1) * MOE_BLOCK
    block_e = jnp.minimum(jnp.sum((pad_end[None, :] <= blk[:, None]).astype(I32), axis=1), N_EXPERTS - 1)
    last_blk = jnp.concatenate([jnp.where(counts > 0, pad_end - MOE_BLOCK, -1), pad_end[-1:]]).astype(I32)
    tile_base = pad_start[None, :] + jnp.cumsum(tile_counts, axis=0) - tile_counts
    base = jnp.broadcast_to(tile_base.astype(F32)[:, :, None], tile_base.shape + (LANES,))
    dest3 = _slots(eidx, base)
    xin = _dispatch(dest3, hrow, last_blk, n_blocks * MOE_BLOCK)
    yout = _experts(xin, block_e, n_used, norm_w, w1, w3, w2, layer)
    nw = final_nw if final_nw is not None else norm_w
    return _combine(dest3, h1, wts, yout, nw, final_nw is not None, split_rows)


def _even_mixers(x, seq, p):
    wi = p["even_w_in"][0]
    z_end = SSD_INNER
    xbc_end = z_end + SSD_CONV_DIM
    dt_end = xbc_end + 2 * SSD_HEADS
    w0 = jnp.concatenate([wi[:, :xbc_end], wi[:, dt_end:], wi[:, xbc_end:dt_end],
                          jnp.zeros((D_MODEL, LANES - 2 * SSD_HEADS), F32)], axis=1).astype(BF16)
    z, xbc, xp, dt = _norm_proj(x, p["norm_mix"][0], w0, (SSD_INNER, SSD_CONV_DIM, POOL_WIDTH, LANES),
                                (BF16, BF16, BF16, F32))
    xs, bm, cm = _ssd_conv(xbc, p["ssd_conv_w"][0], p["ssd_conv_b"][0], seq)
    y_ssd = _ssd_scan(xs, bm, cm, z, dt, p["ssd_dt_bias"][0], p["ssd_A_log"][0], p["ssd_D"][0],
                      p["ssd_norm_w"][0], seq)
    y_pool = _pool_mixer(xp, p["pool_w"][0], p["pool_scale"][0], seq)
    return y_ssd, y_pool


def _odd_mixers(h, seq, p):
    wo = p["odd_w_in"][0]
    qkv0 = 3 * SC_WIDTH + Q_DIM

    def dup_heads(w):
        return jnp.repeat(w.reshape(D_MODEL, N_KV_HEADS, 1, HEAD_DIM), 2, axis=2).reshape(D_MODEL, 2 * KV_DIM)

    w1 = jnp.concatenate([wo[:, :qkv0], dup_heads(wo[:, qkv0:qkv0 + KV_DIM]),
                          dup_heads(wo[:, qkv0 + KV_DIM:])], axis=1).astype(BF16)
    gb, gc, xc, q, kd, vd = _norm_proj(
        h, p["norm_mix"][1], w1, (SC_WIDTH, SC_WIDTH, SC_WIDTH, Q_DIM, 2 * KV_DIM, 2 * KV_DIM), (BF16,) * 6,
        rope_scales=(None, None, None, HEAD_DIM ** -0.5 * LOG2E, 1.0, None), seq=seq)
    y_conv = _short_conv(gb, gc, xc, p["sc_conv_w"][0], seq)
    y_attn = _attention(q, kd, vd, p["attn_sinks"][0], seq)
    return y_conv, y_attn


def _trunk(x, seq, p):
    y_ssd, y_pool = _even_mixers(x, seq, p)
    h, hrow, eidx, wts, hist = _out_router(y_ssd, y_pool, x, p["even_w_out"][0], p["norm_ffn"][0],
                                           p["moe_w_group"][0], p["moe_b_group"][0], p["moe_w_expert"][0],
                                           p["moe_b_expert"][0])
    h = _moe(h, hrow, eidx, wts, hist, p["norm_ffn"][0], p["moe_w1"], p["moe_w3"], p["moe_w2"], 0)
    y_conv, y_attn = _odd_mixers(h, seq, p)
    h, hrow, eidx, wts, hist = _out_router(y_conv, y_attn, h, p["odd_w_out"][0], p["norm_ffn"][1],
                                           p["moe_w_group"][1], p["moe_b_group"][1], p["moe_w_expert"][1],
                                           p["moe_b_expert"][1])
    return _moe(h, hrow, eidx, wts, hist, p["norm_ffn"][1], p["moe_w1"], p["moe_w3"], p["moe_w2"], 1,
                final_nw=p["norm_final"], split_rows=seq[0])


def kernel(x_prompt, x_sample, norm_mix, norm_ffn, norm_final, even_w_in, ssd_conv_w, ssd_conv_b, ssd_A_log,
           ssd_dt_bias, ssd_D, ssd_norm_w, pool_w, pool_scale, even_w_out, odd_w_in, sc_conv_w, attn_sinks,
           odd_w_out, moe_w_group, moe_b_group, moe_w_expert, moe_b_expert, moe_w1, moe_w3, moe_w2):
    p = dict(norm_mix=norm_mix, norm_ffn=norm_ffn, norm_final=norm_final, even_w_in=even_w_in,
             ssd_conv_w=ssd_conv_w, ssd_conv_b=ssd_conv_b, ssd_A_log=ssd_A_log, ssd_dt_bias=ssd_dt_bias,
             ssd_D=ssd_D, ssd_norm_w=ssd_norm_w, pool_w=pool_w, pool_scale=pool_scale, even_w_out=even_w_out,
             odd_w_in=odd_w_in, sc_conv_w=sc_conv_w, attn_sinks=attn_sinks, odd_w_out=odd_w_out,
             moe_w_group=moe_w_group, moe_b_group=moe_b_group, moe_w_expert=moe_w_expert,
             moe_b_expert=moe_b_expert, moe_w1=moe_w1, moe_w3=moe_w3, moe_w2=moe_w2)
    bp, lp, d = x_prompt.shape
    bs, ls, _ = x_sample.shape
    ta = bp * lp
    ya, yb = _trunk((x_prompt.reshape(ta, d), x_sample.reshape(bs * ls, d)), (ta, lp, ls), p)
    return ya.reshape(bp, lp, d), yb.reshape(bs, ls, d)
```

```python
import functools

import jax
import jax.numpy as jnp
from jax import lax
from jax.experimental import pallas as pl
from jax.experimental.pallas import tpu as pltpu

F32 = jnp.float32
BF16 = jnp.bfloat16
I32 = jnp.int32

RMS_EPS = 1e-6
D_MODEL = 1024
LANES = 128
ROW_SUB = D_MODEL // LANES
BF16_ROWS = 16
VMEM_LIMIT = 56 * 1024 * 1024

SSD_HEADS = 24
SSD_HEAD_DIM = 64
SSD_INNER = SSD_HEADS * SSD_HEAD_DIM
SSD_GROUPS = 4
SSD_STATE = 128
SSD_BC = SSD_GROUPS * SSD_STATE
SSD_CONV_DIM = SSD_INNER + 2 * SSD_BC
SSD_TAPS = 5
CHUNK = 128
POOL_WINDOWS = (2, 4, 8, 16)
POOL_WIDTH = 512
SC_WIDTH = 1024
N_Q_HEADS = 16
N_KV_HEADS = 4
HEAD_DIM = 64
Q_DIM = N_Q_HEADS * HEAD_DIM
KV_DIM = N_KV_HEADS * HEAD_DIM
WINDOW = 128
ROPE_THETA = 10000.0
N_GROUPS = 4
PER_GROUP = 8
N_EXPERTS = N_GROUPS * PER_GROUP
D_EXPERT = 512
MOE_BLOCK = 512
ROW_TILE = 512
DMA_UNROLL = 8
COMBINE_PARTS = 4


def _cparams(n_grid, semantics="parallel"):
    return pltpu.CompilerParams(dimension_semantics=(semantics,) * n_grid, vmem_limit_bytes=VMEM_LIMIT)


def _seq_pos(tok0, ta, la, lb):
    in_a = tok0 < ta
    seqlen = jnp.where(in_a, la, lb)
    pos = jnp.where(in_a, tok0 % la, (tok0 - ta) % lb)
    return pos, seqlen


def _row_parts(x):
    return tuple(x) if isinstance(x, (tuple, list)) else (x,)


def _row_in_specs(parts, r, w):
    if len(parts) == 1:
        return [pl.BlockSpec((r, w), lambda i: (i, 0))]
    na = parts[0].shape[0] // r
    return [pl.BlockSpec((r, w), lambda i: (jnp.minimum(i, na - 1), 0)),
            pl.BlockSpec((r, w), lambda i: (jnp.maximum(i - na, 0), 0))]


def _load_rows(refs, na):
    if len(refs) == 1:
        return refs[0][...]
    return jnp.where(pl.program_id(0) < na, refs[0][...], refs[1][...])


def _sigmoid(x):
    return 1.0 / (1.0 + jnp.exp(-x))


def _silu(x):
    return x * _sigmoid(x)


def _softplus(x):
    return jnp.maximum(x, 0.0) + jnp.log1p(jnp.exp(-jnp.abs(x)))


def _split3(x):
    hi = x.astype(BF16)
    r1 = x - hi.astype(F32)
    mid = r1.astype(BF16)
    lo = (r1 - mid.astype(F32)).astype(BF16)
    return hi, mid, lo


def _tri_dot(mask, v):
    m = jnp.where(mask, 1.0, 0.0).astype(BF16)
    hi, mid, lo = _split3(v)
    acc = jnp.dot(m, hi, preferred_element_type=F32)
    acc += jnp.dot(m, mid, preferred_element_type=F32)
    acc += jnp.dot(m, lo, preferred_element_type=F32)
    return acc


def _tn_dot(a, b):
    at = a.astype(F32).T.astype(BF16)
    return jnp.dot(at, b, preferred_element_type=F32)


def _rope(t, cos, sin_signed):
    n = t.shape[1]
    lane = lax.broadcasted_iota(I32, t.shape, 1)
    first = (lane % HEAD_DIM) < HEAD_DIM // 2
    rot = jnp.where(first, pltpu.roll(t, n - HEAD_DIM // 2, 1), pltpu.roll(t, HEAD_DIM // 2, 1))
    reps = n // LANES
    return t * jnp.tile(cos, (1, reps)) + rot * jnp.tile(sin_signed, (1, reps))


def _norm_proj_kernel(*refs, n_x, na, widths, rope_scales):
    x_refs, (nw_ref, w_ref), refs = refs[:n_x], refs[n_x:n_x + 2], refs[n_x + 2:]
    use_rope = any(s is not None for s in rope_scales)
    if use_rope:
        cos_ref, sin_ref = refs[:2]
        out_refs = refs[2:]
    else:
        out_refs = refs
    x = _load_rows(x_refs, na)
    ms = jnp.mean(x * x, axis=-1, keepdims=True)
    xn = ((x * lax.rsqrt(ms + RMS_EPS)) * nw_ref[...]).astype(BF16)
    off = 0
    for wd, o, rs in zip(widths, out_refs, rope_scales):
        for c0 in range(0, wd, 512):
            cw = min(512, wd - c0)
            v = jnp.dot(xn, w_ref[:, off + c0:off + c0 + cw], preferred_element_type=F32)
            if rs is not None:
                v = _rope(v, cos_ref[...], sin_ref[...]) * rs
            o[:, c0:c0 + cw] = v.astype(o.dtype)
        off += wd


def _norm_proj(x, nw, w, widths, dtypes, rope_scales=None, seq=None):
    parts = _row_parts(x)
    d = parts[0].shape[1]
    t = sum(p.shape[0] for p in parts)
    n = w.shape[1]
    assert sum(widths) == n and all(p.shape[0] % ROW_TILE == 0 for p in parts)
    rope_scales = tuple(rope_scales) if rope_scales is not None else (None,) * len(widths)
    in_specs = _row_in_specs(parts, ROW_TILE, d) + [pl.BlockSpec((1, d), lambda i: (0, 0)),
                                                    pl.BlockSpec((d, n), lambda i: (0, 0))]
    args = list(parts) + [nw.reshape(1, d), w]
    if any(s is not None for s in rope_scales):
        ta, la, lb = seq
        cos, sin = _rope_tables(max(la, lb))
        tab = pl.BlockSpec((ROW_TILE, LANES), lambda i: (_seq_pos(i * ROW_TILE, ta, la, lb)[0] // ROW_TILE, 0))
        in_specs += [tab, tab]
        args += [cos, sin]
    return pl.pallas_call(
        functools.partial(_norm_proj_kernel, n_x=len(parts), na=parts[0].shape[0] // ROW_TILE,
                          widths=tuple(widths), rope_scales=rope_scales),
        grid=(t // ROW_TILE,),
        in_specs=in_specs,
        out_specs=[pl.BlockSpec((ROW_TILE, wd), lambda i: (i, 0)) for wd in widths],
        out_shape=[jax.ShapeDtypeStruct((t, wd), dt) for wd, dt in zip(widths, dtypes)],
        compiler_params=_cparams(1),
        name="norm_proj",
    )(*args)


def _halo_specs(r, width, t):
    hb = r // BF16_ROWS
    last = t // BF16_ROWS - 1
    return [pl.BlockSpec((BF16_ROWS, width), lambda i: (jnp.maximum(i * hb - 1, 0), 0)),
            pl.BlockSpec((r, width), lambda i: (i, 0)),
            pl.BlockSpec((BF16_ROWS, width), lambda i: (jnp.minimum((i + 1) * hb, last), 0))]


def _fill_halo_buf(buf, prev_ref, cur_ref, next_ref, r, has_prev, has_next):
    buf[0:BF16_ROWS, :] = jnp.where(has_prev, prev_ref[...].astype(F32), 0.0)
    buf[BF16_ROWS:BF16_ROWS + r, :] = cur_ref[...].astype(F32)
    buf[BF16_ROWS + r:2 * BF16_ROWS + r, :] = jnp.where(has_next, next_ref[...].astype(F32), 0.0)


def _ssd_conv_kernel(prev_ref, cur_ref, next_ref, w_ref, b_ref, xs_ref, bm_ref, cm_ref, buf,
                     *, r, ta, la, lb):
    i = pl.program_id(0)
    pos, seqlen = _seq_pos(i * r, ta, la, lb)
    _fill_halo_buf(buf, prev_ref, cur_ref, next_ref, r, pos > 0, pos + r < seqlen)
    half = SSD_TAPS // 2
    for c0 in range(0, SSD_CONV_DIM, 512):
        for r0 in range(0, r, 128):
            acc = jnp.broadcast_to(b_ref[:, c0:c0 + 512], (128, 512))
            for k in range(SSD_TAPS):
                start = BF16_ROWS + r0 + k - half
                acc = acc + w_ref[k:k + 1, c0:c0 + 512] * buf[start:start + 128, c0:c0 + 512]
            y = _silu(acc).astype(BF16)
            if c0 < SSD_INNER:
                xs_ref[r0:r0 + 128, c0:c0 + 512] = y
            elif c0 < SSD_INNER + SSD_BC:
                bm_ref[r0:r0 + 128, :] = y
            else:
                cm_ref[r0:r0 + 128, :] = y


def _ssd_conv(xbc, conv_w, conv_b, seq):
    t = xbc.shape[0]
    r = ROW_TILE
    return pl.pallas_call(
        functools.partial(_ssd_conv_kernel, r=r, ta=seq[0], la=seq[1], lb=seq[2]),
        grid=(t // r,),
        in_specs=_halo_specs(r, SSD_CONV_DIM, t) + [
            pl.BlockSpec((SSD_TAPS, SSD_CONV_DIM), lambda i: (0, 0)),
            pl.BlockSpec((1, SSD_CONV_DIM), lambda i: (0, 0))],
        out_specs=[pl.BlockSpec((r, SSD_INNER), lambda i: (i, 0)),
                   pl.BlockSpec((r, SSD_BC), lambda i: (i, 0)),
                   pl.BlockSpec((r, SSD_BC), lambda i: (i, 0))],
        out_shape=[jax.ShapeDtypeStruct((t, SSD_INNER), BF16),
                   jax.ShapeDtypeStruct((t, SSD_BC), BF16),
                   jax.ShapeDtypeStruct((t, SSD_BC), BF16)],
        scratch_shapes=[pltpu.VMEM((r + 2 * BF16_ROWS, SSD_CONV_DIM), F32)],
        compiler_params=_cparams(1),
        name="ssd_conv",
    )(xbc, xbc, xbc, conv_w, conv_b.reshape(1, SSD_CONV_DIM))


def _chunk_iotas():
    ri = lax.broadcasted_iota(I32, (CHUNK, CHUNK), 0)
    ci = lax.broadcasted_iota(I32, (CHUNK, CHUNK), 1)
    return ri, ci


def _col(v, lane):
    return jnp.broadcast_to(v[:, lane:lane + 1], (CHUNK, LANES))


def _pair(left, a, b):
    return jnp.where(left, a, b)


def _ssd_prep_kernel(dt_ref, bias_ref, alog_ref, acs_ref, sfx_ref, wc_ref, rows_ref):
    ri, ci = _chunk_iotas()
    fwd_lane = ci < SSD_HEADS
    a_row = -jnp.exp(alog_ref[...])
    for k in range(dt_ref.shape[0] // CHUNK):
        rs = slice(k * CHUNK, (k + 1) * CHUNK)
        dt_all = _softplus(dt_ref[rs, :] + bias_ref[...])
        da = dt_all * a_row
        acs = _tri_dot(ci <= ri, da)
        sfx = _tri_dot(ci >= ri, da)
        acs_ref[rs, :] = acs
        sfx_ref[rs, :] = sfx
        wc_ref[rs, :] = dt_all * jnp.exp(jnp.where(fwd_lane, acs[CHUNK - 1:CHUNK, :] - acs, sfx[0:1, :] - sfx))
        rows_ref[k, 0] = acs.T
        rows_ref[k, 1] = sfx.T
        rows_ref[k, 2] = dt_all.T


def _ssd_bwd_kernel(x_ref, b_ref, c_ref, sfx_ref, wc_ref, out_ref, h_ref, xw_ref, *, n_chunks, ta, la, lb):
    c = n_chunks - 1 - pl.program_id(0)
    pos, seqlen = _seq_pos(c * CHUNK, ta, la, lb)

    @pl.when(pos + CHUNK == seqlen)
    def _():
        h_ref[...] = jnp.zeros_like(h_ref)

    _, ci = _chunk_iotas()
    left = ci < SSD_HEAD_DIM
    sfx = sfx_ref[...]
    wc = wc_ref[...]
    for g in range(SSD_GROUPS):
        gs = slice(g * 384, (g + 1) * 384)
        cg = c_ref[:, g * SSD_STATE:(g + 1) * SSD_STATE]
        bg = b_ref[:, g * SSD_STATE:(g + 1) * SSD_STATE]
        hg = h_ref[:, gs]
        yoff = jnp.dot(cg, hg.astype(BF16), preferred_element_type=F32)
        cds = []
        for jj in range(3):
            j = g * 3 + jj
            ls = slice(j * LANES, (j + 1) * LANES)
            h0 = SSD_HEADS + 2 * j
            es = jnp.exp(_pair(left, _col(sfx, h0), _col(sfx, h0 + 1)))
            wt = _pair(left, _col(wc, h0), _col(wc, h0 + 1))
            xw_ref[:, ls] = (x_ref[:, ls].astype(F32) * wt).astype(BF16)
            out_ref[:, ls] = (yoff[:, jj * LANES:(jj + 1) * LANES] * es).astype(out_ref.dtype)
            cds.append(es[0:1, :])
        cd = jnp.concatenate(cds, axis=1)
        h_ref[:, gs] = hg * cd + _tn_dot(bg, xw_ref[:, gs])


def _ssd_main_kernel(x_ref, b_ref, c_ref, z_ref, ybo_ref, acs_ref, sfx_ref, wc_ref, rows_ref, dskip_ref,
                     gw_ref, out_ref, h_ref, xw_ref, y_ref, *, ta, la, lb):
    pos, _ = _seq_pos(pl.program_id(0) * CHUNK, ta, la, lb)

    @pl.when(pos == 0)
    def _():
        h_ref[...] = jnp.zeros_like(h_ref)

    ri, ci = _chunk_iotas()
    left = ci < SSD_HEAD_DIM
    lower = ci <= ri
    low_s = ci < ri
    up_s = ci > ri
    acs = acs_ref[...]
    sfx = sfx_ref[...]
    wc = wc_ref[...]

    def row(k, h):
        return jnp.broadcast_to(rows_ref[0, k, h:h + 1, :], (CHUNK, CHUNK))

    for g in range(SSD_GROUPS):
        gs = slice(g * 384, (g + 1) * 384)
        cg = c_ref[:, g * SSD_STATE:(g + 1) * SSD_STATE]
        bg = b_ref[:, g * SSD_STATE:(g + 1) * SSD_STATE]
        cb = lax.dot_general(cg, bg, (((1,), (1,)), ((), ())), preferred_element_type=F32)
        hg = h_ref[:, gs]
        yoff = jnp.dot(cg, hg.astype(BF16), preferred_element_type=F32)
        cds = []
        for jj in range(3):
            j = g * 3 + jj
            ls = slice(j * LANES, (j + 1) * LANES)
            xt = x_ref[:, ls]
            xf = xt.astype(F32)
            yd = jnp.zeros((CHUNK, LANES), F32)
            cols = []
            for s in range(2):
                h = 2 * j + s
                hb = SSD_HEADS + h
                colf = _col(acs, h)
                cols.append(colf)
                arg = jnp.where(lower, colf - row(0, h), _col(sfx, hb) - row(1, hb))
                dtf = row(2, h)
                dtb = row(2, hb)
                dsel = jnp.where(low_s, dtf, jnp.where(up_s, dtb, dtf + dtb))
                m = (cb * jnp.exp(arg) * dsel).astype(BF16)
                xm = jnp.where(left if s == 0 else jnp.logical_not(left), xf, 0.0).astype(BF16)
                yd = yd + jnp.dot(m, xm, preferred_element_type=F32)
            es = jnp.exp(_pair(left, cols[0], cols[1]))
            wt = _pair(left, _col(wc, 2 * j), _col(wc, 2 * j + 1))
            xw_ref[:, ls] = (xf * wt).astype(BF16)
            y_ref[:, ls] = (yd + yoff[:, jj * LANES:(jj + 1) * LANES] * es
                            + ybo_ref[:, ls].astype(F32) + xf * dskip_ref[:, ls])
            cds.append(es[CHUNK - 1:CHUNK, :])
        cd = jnp.concatenate(cds, axis=1)
        h_ref[:, gs] = hg * cd + _tn_dot(bg, xw_ref[:, gs])

    y = y_ref[...] * _silu(z_ref[...].astype(F32))
    ms = jnp.mean(y * y, axis=-1, keepdims=True)
    out_ref[...] = ((y * lax.rsqrt(ms + RMS_EPS)) * gw_ref[...]).astype(out_ref.dtype)


def _ssd_scan(xs, bm, cm, z, dt, dt_bias, a_log, d_skip, gnorm_w, seq):
    t = xs.shape[0]
    n = t // CHUNK
    ta, la, lb = seq
    pad = LANES - 2 * SSD_HEADS
    bias = jnp.pad(dt_bias.reshape(1, 2 * SSD_HEADS), ((0, 0), (0, pad)))
    alog = jnp.pad(a_log.reshape(1, 2 * SSD_HEADS), ((0, 0), (0, pad)))
    dexp = jnp.repeat(d_skip, SSD_HEAD_DIM).reshape(1, SSD_INNER)
    small = lambda w: pl.BlockSpec((1, w), lambda i: (0, 0))

    pc = ROW_TILE // CHUNK
    lane_rows = pl.BlockSpec((ROW_TILE, LANES), lambda i: (i, 0))
    acs, sfx, wc, rows = pl.pallas_call(
        _ssd_prep_kernel,
        grid=(t // ROW_TILE,),
        in_specs=[lane_rows, small(LANES), small(LANES)],
        out_specs=[lane_rows, lane_rows, lane_rows,
                   pl.BlockSpec((pc, 3, CHUNK, CHUNK), lambda i: (i, 0, 0, 0))],
        out_shape=[jax.ShapeDtypeStruct((t, LANES), F32)] * 3 + [jax.ShapeDtypeStruct((n, 3, CHUNK, CHUNK), F32)],
        compiler_params=_cparams(1),
        name="ssd_prep",
    )(dt, bias, alog)

    rev = lambda i: (n - 1 - i, 0)
    ybo = pl.pallas_call(
        functools.partial(_ssd_bwd_kernel, n_chunks=n, ta=ta, la=la, lb=lb),
        grid=(n,),
        in_specs=[pl.BlockSpec((CHUNK, SSD_INNER), rev),
                  pl.BlockSpec((CHUNK, SSD_BC), rev),
                  pl.BlockSpec((CHUNK, SSD_BC), rev),
                  pl.BlockSpec((CHUNK, LANES), rev),
                  pl.BlockSpec((CHUNK, LANES), rev)],
        out_specs=pl.BlockSpec((CHUNK, SSD_INNER), rev),
        out_shape=jax.ShapeDtypeStruct((t, SSD_INNER), BF16),
        scratch_shapes=[pltpu.VMEM((SSD_STATE, SSD_INNER), F32),
                        pltpu.VMEM((CHUNK, SSD_INNER), BF16)],
        compiler_params=_cparams(1, "arbitrary"),
        name="ssd_bwd",
    )(xs, bm, cm, sfx, wc)

    fwd = lambda i: (i, 0)
    lane_chunk = pl.BlockSpec((CHUNK, LANES), fwd)
    return pl.pallas_call(
        functools.partial(_ssd_main_kernel, ta=ta, la=la, lb=lb),
        grid=(n,),
        in_specs=[pl.BlockSpec((CHUNK, SSD_INNER), fwd),
                  pl.BlockSpec((CHUNK, SSD_BC), fwd),
                  pl.BlockSpec((CHUNK, SSD_BC), fwd),
                  pl.BlockSpec((CHUNK, SSD_INNER), fwd),
                  pl.BlockSpec((CHUNK, SSD_INNER), fwd),
                  lane_chunk, lane_chunk, lane_chunk,
                  pl.BlockSpec((1, 3, CHUNK, CHUNK), lambda i: (i, 0, 0, 0)),
                  small(SSD_INNER), small(SSD_INNER)],
        out_specs=pl.BlockSpec((CHUNK, SSD_INNER), fwd),
        out_shape=jax.ShapeDtypeStruct((t, SSD_INNER), BF16),
        scratch_shapes=[pltpu.VMEM((SSD_STATE, SSD_INNER), F32),
                        pltpu.VMEM((CHUNK, SSD_INNER), BF16),
                        pltpu.VMEM((CHUNK, SSD_INNER), F32)],
        compiler_params=_cparams(1, "arbitrary"),
        name="ssd_main",
    )(xs, bm, cm, z, ybo, acs, sfx, wc, rows, dexp, gnorm_w.reshape(1, SSD_INNER))


def _pool_kernel(prev_ref, cur_ref, next_ref, w_ref, sc_ref, out_ref, buf, *, r, ta, la, lb):
    i = pl.program_id(0)
    pos, seqlen = _seq_pos(i * r, ta, la, lb)
    _fill_halo_buf(buf, prev_ref, cur_ref, next_ref, r, pos > 0, pos + r < seqlen)
    for gi, w in enumerate(POOL_WINDOWS):
        ls = slice(gi * LANES, (gi + 1) * LANES)
        for r0 in range(0, r, 128):
            base = BF16_ROWS + r0
            acc = buf[base - w // 2:base - w // 2 + 128, ls]
            for k in range(1, w):
                s = base - w // 2 + k
                acc = acc + buf[s:s + 128, ls]
            tpos = pos + r0 + lax.broadcasted_iota(I32, (128, LANES), 0)
            lo = jnp.maximum(tpos - w // 2, 0)
            hi = jnp.minimum(tpos - w // 2 + w, seqlen)
            mean = acc / (hi - lo).astype(F32)
            diff = (mean - buf[base:base + 128, ls]).astype(BF16)
            o = jnp.dot(diff, w_ref[gi], preferred_element_type=F32) * sc_ref[:, ls]
            out_ref[r0:r0 + 128, ls] = o.astype(out_ref.dtype)


def _pool_mixer(xp, pool_w, pool_scale, seq):
    t = xp.shape[0]
    r = ROW_TILE
    return pl.pallas_call(
        functools.partial(_pool_kernel, r=r, ta=seq[0], la=seq[1], lb=seq[2]),
        grid=(t // r,),
        in_specs=_halo_specs(r, POOL_WIDTH, t) + [
            pl.BlockSpec((len(POOL_WINDOWS), LANES, LANES), lambda i: (0, 0, 0)),
            pl.BlockSpec((1, POOL_WIDTH), lambda i: (0, 0))],
        out_specs=pl.BlockSpec((r, POOL_WIDTH), lambda i: (i, 0)),
        out_shape=jax.ShapeDtypeStruct((t, POOL_WIDTH), BF16),
        scratch_shapes=[pltpu.VMEM((r + 2 * BF16_ROWS, POOL_WIDTH), F32)],
        compiler_params=_cparams(1),
        name="pool_mixer",
    )(xp, xp, xp, pool_w.astype(BF16), pool_scale.reshape(1, POOL_WIDTH))


def _short_conv_kernel(gcp, gcc, gcn, xp, xc, xn, gb_ref, w_ref, out_ref, buf, *, r, ta, la, lb):
    i = pl.program_id(0)
    pos, seqlen = _seq_pos(i * r, ta, la, lb)
    has_prev = pos > 0
    has_next = pos + r < seqlen
    prod = lambda a, b: a[...].astype(F32) * b[...].astype(F32)
    buf[0:BF16_ROWS, :] = jnp.where(has_prev, prod(gcp, xp), 0.0)
    buf[BF16_ROWS:BF16_ROWS + r, :] = prod(gcc, xc)
    buf[BF16_ROWS + r:2 * BF16_ROWS + r, :] = jnp.where(has_next, prod(gcn, xn), 0.0)
    for c0 in range(0, SC_WIDTH, 512):
        cs = slice(c0, c0 + 512)
        for r0 in range(0, r, 128):
            base = BF16_ROWS + r0
            acc = w_ref[0:1, cs] * buf[base - 1:base + 127, cs]
            acc = acc + w_ref[1:2, cs] * buf[base:base + 128, cs]
            acc = acc + w_ref[2:3, cs] * buf[base + 1:base + 129, cs]
            out_ref[r0:r0 + 128, cs] = (gb_ref[r0:r0 + 128, cs].astype(F32) * acc).astype(out_ref.dtype)


def _short_conv(gb, gc, xc, conv_w, seq):
    t = gb.shape[0]
    r = ROW_TILE
    halo = _halo_specs(r, SC_WIDTH, t)
    return pl.pallas_call(
        functools.partial(_short_conv_kernel, r=r, ta=seq[0], la=seq[1], lb=seq[2]),
        grid=(t // r,),
        in_specs=halo + halo + [pl.BlockSpec((r, SC_WIDTH), lambda i: (i, 0)),
                                pl.BlockSpec((3, SC_WIDTH), lambda i: (0, 0))],
        out_specs=pl.BlockSpec((r, SC_WIDTH), lambda i: (i, 0)),
        out_shape=jax.ShapeDtypeStruct((t, SC_WIDTH), BF16),
        scratch_shapes=[pltpu.VMEM((r + 2 * BF16_ROWS, SC_WIDTH), F32)],
        compiler_params=_cparams(1),
        name="short_conv",
    )(gc, gc, gc, xc, xc, xc, gb, conv_w)


LOG2E = 1.4426950408889634


def _attn_kernel(q_ref, kp_ref, kc_ref, kn_ref, vp_ref, vc_ref, vn_ref, sink_ref, out_ref, *, ta, la, lb):
    pos, seqlen = _seq_pos(pl.program_id(0) * CHUNK, ta, la, lb)
    has_prev = pos > 0
    has_next = pos + CHUNK < seqlen
    rows = 4 * CHUNK
    qi = lax.broadcasted_iota(I32, (rows, 3 * CHUNK), 0) % CHUNK
    kk = lax.broadcasted_iota(I32, (rows, 3 * CHUNK), 1)
    mask = (kk >= qi) & (kk <= qi + 2 * WINDOW)
    mask = mask & ((kk >= CHUNK) | has_prev) & ((kk < 2 * CHUNK) | has_next)
    bias = jnp.where(mask, 0.0, -jnp.inf)
    lo_q = lax.broadcasted_iota(I32, (2 * CHUNK, LANES), 1) < HEAD_DIM
    lo_v = lax.broadcasted_iota(I32, (3 * CHUNK, LANES), 1) < HEAD_DIM
    rowi = lax.broadcasted_iota(I32, (rows, 1), 0)
    nt = (((1,), (1,)), ((), ()))
    zq = jnp.zeros((2 * CHUNK, LANES), BF16)
    zv = jnp.zeros((3 * CHUNK, LANES), BF16)
    scores, sinks, vds = [], [], []
    for g in range(N_KV_HEADS):
        ls = slice(g * LANES, (g + 1) * LANES)
        kd = jnp.concatenate([kp_ref[:, ls], kc_ref[:, ls], kn_ref[:, ls]], axis=0)
        vds.append(jnp.concatenate([vp_ref[:, ls], vc_ref[:, ls], vn_ref[:, ls]], axis=0))
        q2 = jnp.concatenate([q_ref[:, (2 * g) * LANES:(2 * g + 1) * LANES],
                              q_ref[:, (2 * g + 1) * LANES:(2 * g + 2) * LANES]], axis=0)
        q4 = jnp.concatenate([jnp.where(lo_q, q2, zq), jnp.where(lo_q, zq, q2)], axis=0)
        scores.append(lax.dot_general(q4, kd, nt, preferred_element_type=F32) + bias)
        heads = (4 * g, 4 * g + 2, 4 * g + 1, 4 * g + 3)
        sk = sink_ref[:, heads[3]:heads[3] + 1]
        for k in (2, 1, 0):
            sk = jnp.where(rowi < (k + 1) * CHUNK, sink_ref[:, heads[k]:heads[k] + 1], sk)
        sinks.append(sk * LOG2E)
    s = jnp.concatenate(scores, axis=0)
    sink = jnp.concatenate(sinks, axis=0)
    m = jnp.maximum(jnp.max(s, axis=-1, keepdims=True), sink)
    p = jnp.exp2(s - m)
    inv = 1.0 / (jnp.sum(p, axis=-1, keepdims=True) + jnp.exp2(sink - m))
    p = p.astype(BF16)
    for g in range(N_KV_HEADS):
        pg = p[g * rows:(g + 1) * rows]
        ig = inv[g * rows:(g + 1) * rows]
        half = 2 * CHUNK
        acc = (jnp.dot(pg[0:half], jnp.where(lo_v, vds[g], zv), preferred_element_type=F32) * ig[0:half]
               + jnp.dot(pg[half:rows], jnp.where(lo_v, zv, vds[g]), preferred_element_type=F32) * ig[half:rows])
        out_ref[:, (2 * g) * LANES:(2 * g + 1) * LANES] = acc[0:CHUNK].astype(out_ref.dtype)
        out_ref[:, (2 * g + 1) * LANES:(2 * g + 2) * LANES] = acc[CHUNK:half].astype(out_ref.dtype)


def _rope_tables(lmax):
    inv = 1.0 / (ROPE_THETA ** (jnp.arange(0, HEAD_DIM, 2, dtype=F32) / HEAD_DIM))
    ang = jnp.arange(lmax, dtype=F32)[:, None] * inv[None, :]
    cos, sin = jnp.cos(ang), jnp.sin(ang)
    cos = jnp.concatenate([cos, cos, cos, cos], axis=1)
    sin = jnp.concatenate([-sin, sin, -sin, sin], axis=1)
    return cos, sin


def _attention(q, kd, vd, sinks, seq):
    t = q.shape[0]
    n = t // CHUNK
    ta, la, lb = seq
    nlast = n - 1
    cur = lambda i: (i, 0)
    prv = lambda i: (jnp.maximum(i - 1, 0), 0)
    nxt = lambda i: (jnp.minimum(i + 1, nlast), 0)
    kv = lambda f: pl.BlockSpec((CHUNK, 2 * KV_DIM), f)
    return pl.pallas_call(
        functools.partial(_attn_kernel, ta=ta, la=la, lb=lb),
        grid=(n,),
        in_specs=[pl.BlockSpec((CHUNK, Q_DIM), cur), kv(prv), kv(cur), kv(nxt), kv(prv), kv(cur), kv(nxt),
                  pl.BlockSpec((1, N_Q_HEADS), lambda i: (0, 0))],
        out_specs=pl.BlockSpec((CHUNK, Q_DIM), cur),
        out_shape=jax.ShapeDtypeStruct((t, Q_DIM), BF16),
        compiler_params=_cparams(1),
        name="band_attention",
    )(q, kd, kd, kd, vd, vd, vd, sinks.reshape(1, N_Q_HEADS))


def _out_router_kernel(a_ref, b_ref, *refs, n_h, na):
    h_refs, refs = refs[:n_h], refs[n_h:]
    wa_ref, wb_ref, nw_ref, wr_ref, br_ref, h_out, hrow_out, eidx_out, wts_out, hist_out = refs
    h1 = (_load_rows(h_refs, na) + jnp.dot(a_ref[...], wa_ref[...], preferred_element_type=F32)
          + jnp.dot(b_ref[...], wb_ref[...], preferred_element_type=F32))
    h_out[...] = h1
    hrow_out[...] = h1.reshape(hrow_out.shape)
    ms = jnp.mean(h1 * h1, axis=-1, keepdims=True)
    tn = (h1 * lax.rsqrt(ms + RMS_EPS)) * nw_ref[...]
    t_hi, t_mid, _ = _split3(tn)
    w_hi, w_mid, _ = _split3(wr_ref[...])
    nt = (((1,), (1,)), ((), ()))
    logits = (lax.dot_general(w_hi, t_hi, nt, preferred_element_type=F32)
              + lax.dot_general(w_hi, t_mid, nt, preferred_element_type=F32)
              + lax.dot_general(w_mid, t_hi, nt, preferred_element_type=F32)) + br_ref[...]
    tm = logits.shape[1]
    row = lax.broadcasted_iota(I32, (PER_GROUP, tm), 0)
    gl = jnp.where(row < N_GROUPS, logits[0:PER_GROUP], -jnp.inf)
    gmax = jnp.max(gl, axis=0, keepdims=True)
    gsel = jnp.min(jnp.where(gl == gmax, row, PER_GROUP), axis=0, keepdims=True)
    p_group = 1.0 / jnp.sum(jnp.exp(gl - gmax), axis=0, keepdims=True)
    el = logits[PER_GROUP:2 * PER_GROUP]
    for gi in range(1, N_GROUPS):
        el = jnp.where(gsel == gi, logits[(gi + 1) * PER_GROUP:(gi + 2) * PER_GROUP], el)
    m1 = jnp.max(el, axis=0, keepdims=True)
    i1 = jnp.min(jnp.where(el == m1, row, PER_GROUP), axis=0, keepdims=True)
    el2 = jnp.where(row == i1, -jnp.inf, el)
    m2 = jnp.max(el2, axis=0, keepdims=True)
    i2 = jnp.min(jnp.where(el2 == m2, row, PER_GROUP), axis=0, keepdims=True)
    ratio = jnp.exp(m2 - m1)
    w1 = p_group / (1.0 + ratio)
    w2 = p_group * ratio / (1.0 + ratio)
    ex1 = gsel * PER_GROUP + i1
    ex2 = gsel * PER_GROUP + i2
    eidx_out[...] = jnp.where(row == 0, ex1, jnp.where(row == 1, ex2, 0))
    erow = lax.broadcasted_iota(I32, (N_EXPERTS, tm), 0)
    hits = jnp.where((erow == ex1) | (erow == ex2), 1.0, 0.0)
    hist_out[0] = jnp.broadcast_to(jnp.sum(hits, axis=1, keepdims=True), (N_EXPERTS, LANES))
    wrow = lax.broadcasted_iota(I32, (LANES, tm), 0)
    wmat = jnp.where(wrow == 0, w1, jnp.where(wrow == 1, w2, 0.0))
    wts_out[...] = wmat.T


def _out_router(a, b, h, w_out, nw, w_group, b_group, w_expert, b_expert):
    h_parts = _row_parts(h)
    d = h_parts[0].shape[1]
    t = a.shape[0]
    ka, kb = a.shape[1], b.shape[1]
    wa = w_out[:ka].astype(BF16)
    wb = w_out[ka:].astype(BF16)
    wr = jnp.zeros((LANES, d), F32).at[0:N_GROUPS].set(w_group.T).at[PER_GROUP:PER_GROUP + N_EXPERTS].set(w_expert.T)
    br = jnp.zeros((LANES, 1), F32).at[0:N_GROUPS, 0].set(b_group).at[PER_GROUP:PER_GROUP + N_EXPERTS, 0].set(b_expert)
    r = ROW_TILE
    row_spec = lambda w: pl.BlockSpec((r, w), lambda i: (i, 0))
    const = lambda s: pl.BlockSpec(s, lambda i: (0, 0))
    return pl.pallas_call(
        functools.partial(_out_router_kernel, n_h=len(h_parts), na=h_parts[0].shape[0] // r),
        grid=(t // r,),
        in_specs=[row_spec(ka), row_spec(kb)] + _row_in_specs(h_parts, r, d) + [
            const((ka, d)), const((kb, d)), const((1, d)), const((LANES, d)), const((LANES, 1))],
        out_specs=[row_spec(d), pl.BlockSpec((r, ROW_SUB, LANES), lambda i: (i, 0, 0)),
                   pl.BlockSpec((PER_GROUP, r), lambda i: (0, i)), row_spec(LANES),
                   pl.BlockSpec((1, N_EXPERTS, LANES), lambda i: (i, 0, 0))],
        out_shape=[jax.ShapeDtypeStruct((t, d), F32), jax.ShapeDtypeStruct((t, ROW_SUB, LANES), F32),
                   jax.ShapeDtypeStruct((PER_GROUP, t), I32), jax.ShapeDtypeStruct((t, LANES), F32),
                   jax.ShapeDtypeStruct((t // r, N_EXPERTS, LANES), F32)],
        compiler_params=_cparams(1),
        name="out_proj_router",
    )(a, b, *h_parts, wa, wb, nw.reshape(1, d), wr, br)


def _slot_kernel(eidx_ref, base_ref, dest_out):
    tm = eidx_ref.shape[1]
    erow = lax.broadcasted_iota(I32, (N_EXPERTS, tm), 0)
    ti = lax.broadcasted_iota(I32, (tm, tm), 0)
    tj = lax.broadcasted_iota(I32, (tm, tm), 1)
    before = jnp.where(ti < tj, 1.0, 0.0).astype(BF16)
    base = base_ref[0][:, 0:1]
    slots = []
    for k in range(2):
        oh = jnp.where(eidx_ref[k:k + 1, :] == erow, 1.0, 0.0)
        prefix = jnp.dot(oh.astype(BF16), before, preferred_element_type=F32)
        slots.append(jnp.sum(oh * (prefix + base), axis=0, keepdims=True))
        base = base + jnp.sum(oh, axis=1, keepdims=True)
    row = lax.broadcasted_iota(I32, (PER_GROUP, tm), 0)
    dest_out[0] = jnp.where(row == 0, slots[0], jnp.where(row == 1, slots[1], 0.0)).astype(I32)


def _slots(eidx, base):
    t = eidx.shape[1]
    r = ROW_TILE
    return pl.pallas_call(
        _slot_kernel,
        grid=(t // r,),
        in_specs=[pl.BlockSpec((PER_GROUP, r), lambda i: (0, i)),
                  pl.BlockSpec((1, N_EXPERTS, LANES), lambda i: (i, 0, 0))],
        out_specs=pl.BlockSpec((1, PER_GROUP, r), lambda i: (i, 0, 0)),
        out_shape=jax.ShapeDtypeStruct((t // r, PER_GROUP, r), I32),
        compiler_params=_cparams(1),
        name="moe_slots",
    )(eidx, base)


def _row_dma_start(copy, lo, hi):
    def start(r, c):
        copy(r, 0).start(priority=0)
        copy(r, 1).start(priority=1)
        return c

    lax.fori_loop(lo, hi, start, 0, unroll=DMA_UNROLL)


def _row_dma_wait(copy, lo, hi):
    def wait(r, c):
        copy(r, 0).wait()
        copy(r, 1).wait()
        return c

    lax.fori_loop(lo, hi, wait, 0, unroll=DMA_UNROLL)


def _dispatch_kernel(fill_ref, dest_ref, x_ref, out_ref, zbuf, sem):
    n_slots = out_ref.shape[0]

    @pl.when(pl.program_id(0) == 0)
    def _():
        zbuf[...] = jnp.zeros_like(zbuf)
        starts = [fill_ref[e] for e in range(N_EXPERTS)]
        starts += [fill_ref[N_EXPERTS] + j * MOE_BLOCK for j in range(N_EXPERTS)]
        live = [s >= 0 for s in starts[:N_EXPERTS]] + [s < n_slots for s in starts[N_EXPERTS:]]

        def zcopy(s):
            return pltpu.make_async_copy(zbuf, out_ref.at[pl.ds(s, MOE_BLOCK)], sem)

        for s, ok in zip(starts, live):
            @pl.when(ok)
            def _(s=s):
                zcopy(s).start()
        for s, ok in zip(starts, live):
            @pl.when(ok)
            def _(s=s):
                zcopy(s).wait()

    copy = lambda r, k: pltpu.make_async_copy(x_ref.at[r], out_ref.at[dest_ref[0, k, r]], sem)
    _row_dma_start(copy, 0, x_ref.shape[0])
    _row_dma_wait(copy, 0, x_ref.shape[0])


def _dispatch(dest3, hrow, last_blk, n_slots):
    t = hrow.shape[0]
    r = ROW_TILE
    grid_spec = pltpu.PrefetchScalarGridSpec(
        num_scalar_prefetch=1,
        grid=(t // r,),
        in_specs=[pl.BlockSpec((1, PER_GROUP, r), lambda i, lb: (i, 0, 0), memory_space=pltpu.SMEM),
                  pl.BlockSpec((r, ROW_SUB, LANES), lambda i, lb: (i, 0, 0))],
        out_specs=pl.BlockSpec(memory_space=pl.ANY),
        scratch_shapes=[pltpu.VMEM((MOE_BLOCK, ROW_SUB, LANES), F32), pltpu.SemaphoreType.DMA],
    )
    return pl.pallas_call(
        _dispatch_kernel,
        grid_spec=grid_spec,
        out_shape=jax.ShapeDtypeStruct((n_slots, ROW_SUB, LANES), F32),
        compiler_params=_cparams(1, "arbitrary"),
        name="moe_dispatch",
    )(last_blk, dest3, hrow)


def _expert_kernel(be_ref, nb_ref, x_ref, nw_ref, w1_ref, w3_ref, w2_ref, out_ref, w1b, w3b, w2b):
    i = pl.program_id(0)
    prev = be_ref[jnp.maximum(i - 1, 0)]

    @pl.when((i == 0) | (be_ref[i] != prev))
    def _():
        w1b[...] = w1_ref[0, 0].astype(BF16)
        w3b[...] = w3_ref[0, 0].astype(BF16)
        w2b[...] = w2_ref[0, 0].astype(BF16)

    @pl.when(i < nb_ref[0])
    def _():
        x = x_ref[...].reshape(MOE_BLOCK, D_MODEL)
        ms = jnp.mean(x * x, axis=-1, keepdims=True)
        xn = ((x * lax.rsqrt(ms + RMS_EPS)) * nw_ref[...]).astype(BF16)
        a = jnp.dot(xn, w1b[...], preferred_element_type=F32)
        b = jnp.dot(xn, w3b[...], preferred_element_type=F32)
        mid = (_silu(a) * b).astype(BF16)
        out_ref[...] = jnp.dot(mid, w2b[...], preferred_element_type=F32).reshape(out_ref.shape)

    @pl.when(i >= nb_ref[0])
    def _():
        out_ref[...] = jnp.zeros_like(out_ref)


def _experts(xin, block_e, n_used, nw, w1, w3, w2, layer):
    n_slots = xin.shape[0]
    nb = n_slots // MOE_BLOCK
    d, f = w1.shape[2], w1.shape[3]
    rows = (MOE_BLOCK, ROW_SUB, LANES)
    grid_spec = pltpu.PrefetchScalarGridSpec(
        num_scalar_prefetch=2,
        grid=(nb,),
        in_specs=[pl.BlockSpec(rows, lambda i, be, nu: (jnp.minimum(i, nu[0] - 1), 0, 0)),
                  pl.BlockSpec((1, d), lambda i, be, nu: (0, 0)),
                  pl.BlockSpec((1, 1, d, f), lambda i, be, nu: (layer, be[i], 0, 0)),
                  pl.BlockSpec((1, 1, d, f), lambda i, be, nu: (layer, be[i], 0, 0)),
                  pl.BlockSpec((1, 1, f, d), lambda i, be, nu: (layer, be[i], 0, 0))],
        out_specs=pl.BlockSpec(rows, lambda i, be, nu: (i, 0, 0)),
        scratch_shapes=[pltpu.VMEM((d, f), BF16), pltpu.VMEM((d, f), BF16), pltpu.VMEM((f, d), BF16)],
    )
    return pl.pallas_call(
        _expert_kernel,
        grid_spec=grid_spec,
        out_shape=jax.ShapeDtypeStruct((n_slots, ROW_SUB, LANES), F32),
        compiler_params=_cparams(1, "arbitrary"),
        name="moe_experts",
    )(block_e, n_used, xin, nw.reshape(1, d), w1, w3, w2)


def _combine_kernel(dest_ref, h_ref, wts_ref, y_ref, nw_ref, *refs, n_out, na, final_norm):
    out_refs, (gbuf, sems) = refs[:n_out], refs[n_out:]
    tm = h_ref.shape[0]
    part = tm // COMBINE_PARTS

    def copies(p):
        return lambda r, k: pltpu.make_async_copy(y_ref.at[dest_ref[0, k, r]], gbuf.at[k, r], sems.at[p])

    for p in range(COMBINE_PARTS):
        _row_dma_start(copies(p), p * part, (p + 1) * part)
    for p in range(COMBINE_PARTS):
        rs = slice(p * part, (p + 1) * part)
        _row_dma_wait(copies(p), p * part, (p + 1) * part)
        wts = wts_ref[rs, :]
        g0 = gbuf[0, rs].reshape(part, D_MODEL)
        g1 = gbuf[1, rs].reshape(part, D_MODEL)
        out = h_ref[rs, :] + (wts[:, 0:1] * g0 + wts[:, 1:2] * g1)
        if final_norm:
            ms = jnp.mean(out * out, axis=-1, keepdims=True)
            out = (out * lax.rsqrt(ms + RMS_EPS)) * nw_ref[...]
        if n_out == 1:
            out_refs[0][rs, :] = out
        else:
            @pl.when(pl.program_id(0) < na)
            def _(out=out, rs=rs):
                out_refs[0][rs, :] = out

            @pl.when(pl.program_id(0) >= na)
            def _(out=out, rs=rs):
                out_refs[1][rs, :] = out


def _combine(dest3, h, wts, yout, nw, final_norm, split_rows=None):
    t, d = h.shape
    r = ROW_TILE
    if split_rows is None:
        na = t // r
        out_specs = [pl.BlockSpec((r, d), lambda i: (i, 0))]
        out_shape = [jax.ShapeDtypeStruct((t, d), F32)]
    else:
        na = split_rows // r
        out_specs = [pl.BlockSpec((r, d), lambda i: (jnp.minimum(i, na - 1), 0)),
                     pl.BlockSpec((r, d), lambda i: (jnp.maximum(i - na, 0), 0))]
        out_shape = [jax.ShapeDtypeStruct((split_rows, d), F32), jax.ShapeDtypeStruct((t - split_rows, d), F32)]
    outs = pl.pallas_call(
        functools.partial(_combine_kernel, n_out=len(out_specs), na=na, final_norm=final_norm),
        grid=(t // r,),
        in_specs=[pl.BlockSpec((1, PER_GROUP, r), lambda i: (i, 0, 0), memory_space=pltpu.SMEM),
                  pl.BlockSpec((r, d), lambda i: (i, 0)),
                  pl.BlockSpec((r, LANES), lambda i: (i, 0)),
                  pl.BlockSpec(memory_space=pl.ANY),
                  pl.BlockSpec((1, d), lambda i: (0, 0))],
        out_specs=out_specs,
        out_shape=out_shape,
        scratch_shapes=[pltpu.VMEM((2, r, ROW_SUB, LANES), F32), pltpu.SemaphoreType.DMA((COMBINE_PARTS,))],
        compiler_params=_cparams(1, "arbitrary"),
        name="moe_combine",
    )(dest3, h, wts, yout, nw.reshape(1, d))
    return outs[0] if split_rows is None else tuple(outs)


def _moe(h1, hrow, eidx, wts, hist, norm_w, w1, w3, w2, layer, final_nw=None, split_rows=None):
    t = h1.shape[0]
    tile_counts = hist[:, :, 0].astype(I32)
    counts = jnp.sum(tile_counts, axis=0)
    padded = (counts + MOE_BLOCK - 1) // MOE_BLOCK * MOE_BLOCK
    pad_end = jnp.cumsum(padded)
    pad_start = pad_end - padded
    n_blocks = -(-(2 * t) // MOE_BLOCK) + N_EXPERTS
    n_used = (pad_end[-1] // MOE_BLOCK).astype(I32).reshape(1)
    blk = jnp.minimum(jnp.arange(n_blocks, dtype=I32), n_used[0] - 1) * MOE_BLOCK
    block_e = jnp.minimum(jnp.sum((pad_end[None, :] <= blk[:, None]).astype(I32), axis=1), N_EXPERTS - 1)
    last_blk = jnp.concatenate([jnp.where(counts > 0, pad_end - MOE_BLOCK, -1), pad_end[-1:]]).astype(I32)
    tile_base = pad_start[None, :] + jnp.cumsum(tile_counts, axis=0) - tile_counts
    base = jnp.broadcast_to(tile_base.astype(F32)[:, :, None], tile_base.shape + (LANES,))
    dest3 = _slots(eidx, base)
    xin = _dispatch(dest3, hrow, last_blk, n_blocks * MOE_BLOCK)
    yout = _experts(xin, block_e, n_used, norm_w, w1, w3, w2, layer)
    nw = final_nw if final_nw is not None else norm_w
    return _combine(dest3, h1, wts, yout, nw, final_nw is not None, split_rows)


def _even_mixers(x, seq, p):
    wi = p["even_w_in"][0]
    z_end = SSD_INNER
    xbc_end = z_end + SSD_CONV_DIM
    dt_end = xbc_end + 2 * SSD_HEADS
    w0 = jnp.concatenate([wi[:, :xbc_end], wi[:, dt_end:], wi[:, xbc_end:dt_end],
                          jnp.zeros((D_MODEL, LANES - 2 * SSD_HEADS), F32)], axis=1).astype(BF16)
    z, xbc, xp, dt = _norm_proj(x, p["norm_mix"][0], w0, (SSD_INNER, SSD_CONV_DIM, POOL_WIDTH, LANES),
                                (BF16, BF16, BF16, F32))
    xs, bm, cm = _ssd_conv(xbc, p["ssd_conv_w"][0], p["ssd_conv_b"][0], seq)
    y_ssd = _ssd_scan(xs, bm, cm, z, dt, p["ssd_dt_bias"][0], p["ssd_A_log"][0], p["ssd_D"][0],
                      p["ssd_norm_w"][0], seq)
    y_pool = _pool_mixer(xp, p["pool_w"][0], p["pool_scale"][0], seq)
    return y_ssd, y_pool


def _odd_mixers(h, seq, p):
    wo = p["odd_w_in"][0]
    qkv0 = 3 * SC_WIDTH + Q_DIM

    def dup_heads(w):
        return jnp.repeat(w.reshape(D_MODEL, N_KV_HEADS, 1, HEAD_DIM), 2, axis=2).reshape(D_MODEL, 2 * KV_DIM)

    w1 = jnp.concatenate([wo[:, :qkv0], dup_heads(wo[:, qkv0:qkv0 + KV_DIM]),
                          dup_heads(wo[:, qkv0 + KV_DIM:])], axis=1).astype(BF16)
    gb, gc, xc, q, kd, vd = _norm_proj(
        h, p["norm_mix"][1], w1, (SC_WIDTH, SC_WIDTH, SC_WIDTH, Q_DIM, 2 * KV_DIM, 2 * KV_DIM), (BF16,) * 6,
        rope_scales=(None, None, None, HEAD_DIM ** -0.5 * LOG2E, 1.0, None), seq=seq)
    y_conv = _short_conv(gb, gc, xc, p["sc_conv_w"][0], seq)
    y_attn = _attention(q, kd, vd, p["attn_sinks"][0], seq)
    return y_conv, y_attn


def _trunk(x, seq, p):
    y_ssd, y_pool = _even_mixers(x, seq, p)
    h, hrow, eidx, wts, hist = _out_router(y_ssd, y_pool, x, p["even_w_out"][0], p["norm_ffn"][0],
                                           p["moe_w_group"][0], p["moe_b_group"][0], p["moe_w_expert"][0],
                                           p["moe_b_expert"][0])
    h = _moe(h, hrow, eidx, wts, hist, p["norm_ffn"][0], p["moe_w1"], p["moe_w3"], p["moe_w2"], 0)
    y_conv, y_attn = _odd_mixers(h, seq, p)
    h, hrow, eidx, wts, hist = _out_router(y_conv, y_attn, h, p["odd_w_out"][0], p["norm_ffn"][1],
                                           p["moe_w_group"][1], p["moe_b_group"][1], p["moe_w_expert"][1],
                                           p["moe_b_expert"][1])
    return _moe(h, hrow, eidx, wts, hist, p["norm_ffn"][1], p["moe_w1"], p["moe_w3"], p["moe_w2"], 1,
                final_nw=p["norm_final"], split_rows=seq[0])


def kernel(x_prompt, x_sample, norm_mix, norm_ffn, norm_final, even_w_in, ssd_conv_w, ssd_conv_b, ssd_A_log,
           ssd_dt_bias, ssd_D, ssd_norm_w, pool_w, pool_scale, even_w_out, odd_w_in, sc_conv_w, attn_sinks,
           odd_w_out, moe_w_group, moe_b_group, moe_w_expert, moe_b_expert, moe_w1, moe_w3, moe_w2):
    p = dict(norm_mix=norm_mix, norm_ffn=norm_ffn, norm_final=norm_final, even_w_in=even_w_in,
             ssd_conv_w=ssd_conv_w, ssd_conv_b=ssd_conv_b, ssd_A_log=ssd_A_log, ssd_dt_bias=ssd_dt_bias,
             ssd_D=ssd_D, ssd_norm_w=ssd_norm_w, pool_w=pool_w, pool_scale=pool_scale, even_w_out=even_w_out,
             odd_w_in=odd_w_in, sc_conv_w=sc_conv_w, attn_sinks=attn_sinks, odd_w_out=odd_w_out,
             moe_w_group=moe_w_group, moe_b_group=moe_b_group, moe_w_expert=moe_w_expert,
             moe_b_expert=moe_b_expert, moe_w1=moe_w1, moe_w3=moe_w3, moe_w2=moe_w2)
    bp, lp, d = x_prompt.shape
    bs, ls, _ = x_sample.shape
    ta = bp * lp
    ya, yb = _trunk((x_prompt.reshape(ta, d), x_sample.reshape(bs * ls, d)), (ta, lp, ls), p)
    return ya.reshape(bp, lp, d), yb.reshape(bs, ls, d)
```
